```python
import math
import jax, jax.numpy as jnp
from jax import lax
import numpy as np

D_MODEL = 4096
BATCH = 4
SEQ = 2048
DEPTH = 2
DEC_BATCH = 8
DEC_SEQ = 4
PAST_LEN = 16384
PAGE_SIZE = 128

HEAD_DIM = 128
DILATED_PAIRS = ((128, 1), (512, 4), (2048, 16))
N_GROUPS = len(DILATED_PAIRS)
N_A_HEADS = D_MODEL // (2 * HEAD_DIM)
N_B_HEADS = D_MODEL // HEAD_DIM
N_KV_HEADS = 8
GQ = N_B_HEADS // N_KV_HEADS
D_FF = 4 * D_MODEL
N_BUCKETS = 32
MAX_DISTANCE = 2048
RMS_EPS = 1e-6
Q_BLOCK = 128
SB_OFFSET = -7.0

kernel_name = 'dilated_stickbreak_yoco_step'


def rms_norm(x, g):
    x32 = x.astype(jnp.float32)
    y = x32 * lax.rsqrt(jnp.mean(x32 * x32, axis=-1, keepdims=True) + RMS_EPS)
    return (y * g.astype(jnp.float32)).astype(x.dtype)


def rel_bucket(dist):
    max_exact = N_BUCKETS // 2
    d32 = jnp.maximum(dist, 1).astype(jnp.float32)
    large = max_exact + (jnp.log(d32 / max_exact) / math.log(MAX_DISTANCE / max_exact)
                         * (N_BUCKETS - max_exact)).astype(jnp.int32)
    large = jnp.minimum(large, N_BUCKETS - 1)
    return jnp.where(dist < max_exact, dist, large)


def mixer_a_project(h, w_in, g_q, g_k):
    b, t, _ = h.shape
    qkv = (h @ w_in).reshape(b, t, N_GROUPS, 3, N_A_HEADS, HEAD_DIM)
    q = rms_norm(qkv[:, :, :, 0], g_q)
    k = rms_norm(qkv[:, :, :, 1], g_k)
    return q, k, qkv[:, :, :, 2]


def dilated_prompt(q, k, v, bias_g, window, dilation):
    b, t, h, dh = q.shape
    nb = window // dilation
    L = t // dilation
    n_blk = -(-L // nb)
    lp = n_blk * nb

    def to_sub(a):
        a = a.reshape(b, L, dilation, h, dh).transpose(0, 2, 1, 3, 4)
        a = jnp.pad(a, ((0, 0), (0, 0), (0, lp - L), (0, 0), (0, 0)))
        return a.reshape(b, dilation, n_blk, nb, h, dh)

    def with_prev(a):
        prev = jnp.pad(a[:, :, :-1], ((0, 0), (0, 0), (1, 0), (0, 0), (0, 0), (0, 0)))
        return jnp.concatenate([prev, a], axis=3)

    qs = to_sub(q)
    kb = with_prev(to_sub(k))
    vb = with_prev(to_sub(v))
    a_idx = jnp.arange(nb)[:, None]
    b_idx = jnp.arange(2 * nb)[None, :]
    m = nb + a_idx - b_idx
    key_sub = jnp.arange(n_blk)[:, None, None] * nb - nb + b_idx[None]
    valid = ((m >= 0) & (m <= nb))[None] & (key_sub >= 0)
    bias = bias_g[rel_bucket(jnp.clip(m, 0) * dilation)].transpose(2, 0, 1)
    logits = jnp.einsum('bdnqhe,bdnkhe->bdnhqk', qs, kb, preferred_element_type=jnp.float32)
    logits = logits * (HEAD_DIM ** -0.5) + bias[None, None, None]
    logits = jnp.where(valid[None, None, :, None], logits, -jnp.inf)
    lse = jax.nn.logsumexp(logits, axis=-1)
    p = jnp.exp(logits - lse[..., None]).astype(v.dtype)
    o = jnp.einsum('bdnhqk,bdnkhe->bdnqhe', p, vb)
    o = o.reshape(b, dilation, lp, h, dh)[:, :, :L].transpose(0, 2, 1, 3, 4).reshape(b, t, h, dh)
    lse = lse.transpose(0, 1, 2, 4, 3).reshape(b, dilation, lp, h)[:, :, :L]
    lse = lse.transpose(0, 2, 1, 3).reshape(b, t, h)
    return o, lse


def dilated_sample(q, k_all, v_all, bias_g, window, dilation):
    s = q.shape[1]
    wb = k_all.shape[1] - s
    nb = window // dilation
    m = jnp.arange(nb + 1)
    idx = wb + jnp.arange(s)[:, None] - dilation * m[None, :]
    valid = idx >= 0
    idx = jnp.clip(idx, 0)
    kg = k_all[:, idx]
    vg = v_all[:, idx]
    bias = bias_g[rel_bucket(m * dilation)].T
    logits = jnp.einsum('bqhe,bqkhe->bqhk', q, kg, preferred_element_type=jnp.float32)
    logits = logits * (HEAD_DIM ** -0.5) + bias[None, None]
    logits = jnp.where(valid[None, :, None, :], logits, -jnp.inf)
    lse = jax.nn.logsumexp(logits, axis=-1)
    p = jnp.exp(logits - lse[..., None]).astype(v_all.dtype)
    return jnp.einsum('bqhk,bqkhe->bqhe', p, vg), lse


def merge_groups(outs, lses, w_out):
    o = jnp.stack(outs, axis=2)
    lse = jnp.stack(lses, axis=2)
    wts = jax.nn.softmax(lse, axis=2).astype(o.dtype)
    o = jnp.einsum('btgh,btghe->bthe', wts, o)
    return o.reshape(o.shape[0], o.shape[1], -1) @ w_out


def stick_breaking_block(q, k, v, sb_bias, q_pos0):
    b, tq = q.shape[:2]
    tk = k.shape[1]
    qg = q.reshape(b, tq, N_KV_HEADS, GQ, HEAD_DIM)
    z = jnp.einsum('bqkgd,bskd->bkgqs', qg, k, preferred_element_type=jnp.float32) * (HEAD_DIM ** -0.5)
    z = z + sb_bias.astype(jnp.float32).reshape(N_KV_HEADS, GQ)[None, :, :, None, None]
    causal = jnp.arange(tk)[None, :] < (q_pos0 + jnp.arange(tq))[:, None]
    log_1mb = jnp.where(causal, jax.nn.log_sigmoid(-z), 0.0)
    after = lax.cumsum(log_1mb, axis=4, reverse=True) - log_1mb
    a = jnp.where(causal, jnp.exp(jax.nn.log_sigmoid(z) + after), 0.0)
    o = jnp.einsum('bkgqs,bskd->bqkgd', a.astype(v.dtype), v)
    return o.reshape(b, tq, N_B_HEADS, HEAD_DIM)


def stick_breaking_sweep(q, k, v, sb_bias, past_len):
    tq = q.shape[1]
    outs = []
    for s in range(0, tq, Q_BLOCK):
        e = min(s + Q_BLOCK, tq)
        outs.append(stick_breaking_block(q[:, s:e], k[:, :past_len + e], v[:, :past_len + e], sb_bias, past_len + s))
    return jnp.concatenate(outs, axis=1)


def sq_relu_mlp(h, w_up, w_down):
    return jnp.square(jax.nn.relu(h @ w_up)) @ w_down


def setup_inputs(seed: int = 0) -> dict:
    key = jax.random.key(seed)
    keys = jax.random.split(key, 21)
    n_a = DEPTH // 2
    n_b = DEPTH - n_a
    n_pages = PAST_LEN // PAGE_SIZE
    n_pool = (DEC_BATCH * n_pages * 5) // 4
    a_cols = N_GROUPS * 3 * N_A_HEADS * HEAD_DIM
    a_width = N_A_HEADS * HEAD_DIM
    b_width = N_B_HEADS * HEAD_DIM
    kv_cols = 2 * N_KV_HEADS * HEAD_DIM
    wins = [min(w, PAST_LEN) for w, _ in DILATED_PAIRS]

    def normal(k, shape, scale):
        return jax.random.normal(k, shape, jnp.float32) * scale

    def gain(k, shape):
        return 1.0 + 0.02 * jax.random.normal(k, shape, jnp.float32)

    page_table = jax.random.permutation(keys[6], n_pool)[: DEC_BATCH * n_pages]
    page_table = page_table.reshape(DEC_BATCH, n_pages).astype(jnp.int32)
    return {
        'x_prompt': normal(keys[0], (BATCH, SEQ, D_MODEL), 1.0),
        'x_sample': normal(keys[1], (DEC_BATCH, DEC_SEQ, D_MODEL), 1.0),
        'cache_win_g0': normal(keys[2], (n_a, DEC_BATCH, wins[0], 2, N_A_HEADS, HEAD_DIM), 1.0),
        'cache_win_g1': normal(keys[3], (n_a, DEC_BATCH, wins[1], 2, N_A_HEADS, HEAD_DIM), 1.0),
        'cache_win_g2': normal(keys[4], (n_a, DEC_BATCH, wins[2], 2, N_A_HEADS, HEAD_DIM), 1.0),
        'cache_kv': normal(keys[5], (n_pool, PAGE_SIZE, 2, N_KV_HEADS, HEAD_DIM), 1.0),
        'page_table': page_table,
        'bias_table': normal(keys[7], (N_BUCKETS, N_GROUPS, N_A_HEADS), 0.5),
        'norm_mix': gain(keys[8], (DEPTH, D_MODEL)),
        'norm_mlp': gain(keys[9], (DEPTH, D_MODEL)),
        'w_a_in': normal(keys[10], (n_a, D_MODEL, a_cols), D_MODEL ** -0.5),
        'g_a_q': gain(keys[11], (n_a, HEAD_DIM)),
        'g_a_k': gain(keys[12], (n_a, HEAD_DIM)),
        'w_a_out': normal(keys[13], (n_a, a_width, D_MODEL), a_width ** -0.5),
        'norm_kv': gain(keys[14], (D_MODEL,)),
        'w_kv': normal(keys[15], (D_MODEL, kv_cols), D_MODEL ** -0.5),
        'w_b_q': normal(keys[16], (n_b, D_MODEL, b_width), D_MODEL ** -0.5),
        'b_sb': SB_OFFSET + normal(keys[20], (n_b, N_B_HEADS), 0.3),
        'w_b_out': normal(keys[17], (n_b, b_width, D_MODEL), b_width ** -0.5),
        'w_up': normal(keys[18], (DEPTH, D_MODEL, D_FF), D_MODEL ** -0.5),
        'w_down': normal(keys[19], (DEPTH, D_FF, D_MODEL), D_FF ** -0.5),
    }


def reference(x_prompt, x_sample, cache_win_g0, cache_win_g1, cache_win_g2, cache_kv, page_table,
              bias_table, norm_mix, norm_mlp, w_a_in, g_a_q, g_a_k, w_a_out, norm_kv, w_kv,
              w_b_q, b_sb, w_b_out, w_up, w_down):
    n_a = DEPTH // 2
    win_caches = (cache_win_g0, cache_win_g1, cache_win_g2)
    dec_b, n_pages = page_table.shape
    past_len = n_pages * PAGE_SIZE
    past_kv = cache_kv[page_table].reshape(dec_b, past_len, 2, N_KV_HEADS, HEAD_DIM)

    xp, xs = x_prompt, x_sample
    bp, tp = xp.shape[:2]
    bs, ts = xs.shape[:2]
    new_win_p = [[] for _ in range(N_GROUPS)]
    new_win_s = [[] for _ in range(N_GROUPS)]
    kv_p = None
    kv_s = None
    for l in range(DEPTH):
        hp = rms_norm(xp, norm_mix[l])
        hs = rms_norm(xs, norm_mix[l])
        if l < n_a:
            qp, kp, vp = mixer_a_project(hp, w_a_in[l], g_a_q[l], g_a_k[l])
            qs, ks, vs = mixer_a_project(hs, w_a_in[l], g_a_q[l], g_a_k[l])
            outs_p, lses_p, outs_s, lses_s = [], [], [], []
            for g, (window, dil) in enumerate(DILATED_PAIRS):
                bias_g = bias_table[:, g]
                o, lse = dilated_prompt(qp[:, :, g], kp[:, :, g], vp[:, :, g], bias_g, window, dil)
                outs_p.append(o)
                lses_p.append(lse)
                buf = win_caches[g][l]
                k_all = jnp.concatenate([buf[:, :, 0], ks[:, :, g]], axis=1)
                v_all = jnp.concatenate([buf[:, :, 1], vs[:, :, g]], axis=1)
                o, lse = dilated_sample(qs[:, :, g], k_all, v_all, bias_g, window, dil)
                outs_s.append(o)
                lses_s.append(lse)
                wp = min(window, tp)
                wb = buf.shape[1]
                new_win_p[g].append(jnp.stack([kp[:, tp - wp:, g], vp[:, tp - wp:, g]], axis=2))
                new_win_s[g].append(jnp.stack([k_all[:, -wb:], v_all[:, -wb:]], axis=2))
            xp = xp + merge_groups(outs_p, lses_p, w_a_out[l])
            xs = xs + merge_groups(outs_s, lses_s, w_a_out[l])
        else:
            j = l - n_a
            qp = (hp @ w_b_q[j]).reshape(bp, tp, N_B_HEADS, HEAD_DIM)
            qs = (hs @ w_b_q[j]).reshape(bs, ts, N_B_HEADS, HEAD_DIM)
            op = stick_breaking_sweep(qp, kv_p[:, :, 0], kv_p[:, :, 1], b_sb[j], 0)
            k_cat = jnp.concatenate([past_kv[:, :, 0], kv_s[:, :, 0]], axis=1)
            v_cat = jnp.concatenate([past_kv[:, :, 1], kv_s[:, :, 1]], axis=1)
            os_ = stick_breaking_sweep(qs, k_cat, v_cat, b_sb[j], past_len)
            xp = xp + op.reshape(bp, tp, -1) @ w_b_out[j]
            xs = xs + os_.reshape(bs, ts, -1) @ w_b_out[j]
        xp = xp + sq_relu_mlp(rms_norm(xp, norm_mlp[l]), w_up[l], w_down[l])
        xs = xs + sq_relu_mlp(rms_norm(xs, norm_mlp[l]), w_up[l], w_down[l])
        if l == n_a - 1:
            kv_p = (rms_norm(xp, norm_kv) @ w_kv).reshape(bp, tp, 2, N_KV_HEADS, HEAD_DIM)
            kv_s = (rms_norm(xs, norm_kv) @ w_kv).reshape(bs, ts, 2, N_KV_HEADS, HEAD_DIM)

    win_p0 = jnp.stack(new_win_p[0], axis=0)
    win_p1 = jnp.stack(new_win_p[1], axis=0)
    win_p2 = jnp.stack(new_win_p[2], axis=0)
    win_s0 = jnp.stack(new_win_s[0], axis=0)
    win_s1 = jnp.stack(new_win_s[1], axis=0)
    win_s2 = jnp.stack(new_win_s[2], axis=0)
    return (xp, xs, win_p0, win_p1, win_p2, win_s0, win_s1, win_s2, kv_p, kv_s)
```

```python
import functools
import math

import jax
import jax.numpy as jnp
from jax import lax
from jax.experimental import pallas as pl
from jax.experimental.pallas import tpu as pltpu

F32 = jnp.float32
BF16 = jnp.bfloat16

D_MODEL = 4096
HEAD_DIM = 128
DILATED_PAIRS = ((128, 1), (512, 4), (2048, 16))
N_GROUPS = len(DILATED_PAIRS)
N_A_HEADS = D_MODEL // (2 * HEAD_DIM)
A_WIDTH = N_A_HEADS * HEAD_DIM
N_B_HEADS = D_MODEL // HEAD_DIM
N_KV_HEADS = 8
GQ = N_B_HEADS // N_KV_HEADS
KV_WIDTH = N_KV_HEADS * HEAD_DIM
N_BUCKETS = 32
MAX_DISTANCE = 2048
RMS_EPS = 1e-6
PAGE_SIZE = 128
BAND = 128
ATT_SCALE = HEAD_DIM ** -0.5
MASKED = -1e30

V7X_VMEM_LIMIT_BYTES = 56 * 1024 * 1024
NORM_ROWS = 16


def _params(*semantics):
    return pltpu.CompilerParams(dimension_semantics=semantics,
                                vmem_limit_bytes=V7X_VMEM_LIMIT_BYTES)


def _norm_matmul_body(x_ref, g_ref, w_ref, cg_ref, o_ref, xn_ref, *, mode, tm, tn):
    j = pl.program_id(1)

    @pl.when(j == 0)
    def _():
        g = g_ref[...]

        def rows_step(c, carry):
            rows = pl.ds(pl.multiple_of(c * NORM_ROWS, NORM_ROWS), NORM_ROWS)
            x = x_ref[rows, :]
            ms = jnp.mean(x * x, axis=-1, keepdims=True)
            xn_ref[rows, :] = (x * lax.rsqrt(ms + RMS_EPS) * g).astype(BF16)
            return carry

        lax.fori_loop(0, tm // NORM_ROWS, rows_step, 0)

    y = jnp.dot(xn_ref[...], w_ref[...], preferred_element_type=F32)
    if mode == "relu2":
        r = jnp.maximum(y, 0.0)
        o_ref[...] = (r * r).astype(o_ref.dtype)
    elif mode == "plain":
        o_ref[...] = y.astype(o_ref.dtype)
    else:
        o_ref[...] = y
        kind = (j * tn // A_WIDTH) % 3
        rc = min(tm, 64)

        @pl.when(kind != 2)
        def _():
            for c in range(tn // HEAD_DIM):
                cols = slice(c * HEAD_DIM, (c + 1) * HEAD_DIM)
                cg = cg_ref[:, cols]

                def head_step(i, carry):
                    rows = pl.ds(pl.multiple_of(i * rc, rc), rc)
                    t = o_ref[rows, cols]
                    ms = jnp.mean(t * t, axis=-1, keepdims=True)
                    o_ref[rows, cols] = t * lax.rsqrt(ms + RMS_EPS) * cg
                    return carry

                lax.fori_loop(0, tm // rc, head_step, 0)


def _norm_matmul(x, gain, w, *, mode, out_dtype, tm, tn, col_gain=None):
    m, k = x.shape
    n = w.shape[1]
    if col_gain is None:
        col_gain = jnp.ones((1, n), F32)
    body = functools.partial(_norm_matmul_body, mode=mode, tm=tm, tn=tn)
    return pl.pallas_call(
        body,
        grid=(m // tm, n // tn),
        in_specs=[
            pl.BlockSpec((tm, k), lambda i, j: (i, 0), pipeline_mode=pl.Buffered(1)),
            pl.BlockSpec((1, k), lambda i, j: (0, 0)),
            pl.BlockSpec((k, tn), lambda i, j: (0, j)),
            pl.BlockSpec((1, tn), lambda i, j: (0, j)),
        ],
        out_specs=pl.BlockSpec((tm, tn), lambda i, j: (i, j)),
        out_shape=jax.ShapeDtypeStruct((m, n), out_dtype),
        scratch_shapes=[pltpu.VMEM((tm, k), BF16)],
        compiler_params=_params("arbitrary", "arbitrary"),
        name=f"norm_matmul_{mode}_{tm}",
    )(x, gain.reshape(1, k), w, col_gain)


def _matmul_res_body(a_ref, w_ref, r_ref, o_ref):
    kk = pl.program_id(2)
    d = jnp.dot(a_ref[...], w_ref[...], preferred_element_type=F32)

    @pl.when(kk == 0)
    def _():
        o_ref[...] = r_ref[...] + d

    @pl.when(kk != 0)
    def _():
        o_ref[...] += d


def _matmul_res(a, w, res, *, tm, tn, tk):
    m, k = a.shape
    n = w.shape[1]
    return pl.pallas_call(
        _matmul_res_body,
        grid=(m // tm, n // tn, k // tk),
        in_specs=[
            pl.BlockSpec((tm, tk), lambda i, j, kk: (i, kk)),
            pl.BlockSpec((tk, tn), lambda i, j, kk: (kk, j)),
            pl.BlockSpec((tm, tn), lambda i, j, kk: (i, j)),
        ],
        out_specs=pl.BlockSpec((tm, tn), lambda i, j, kk: (i, j)),
        out_shape=jax.ShapeDtypeStruct((m, n), F32),
        compiler_params=_params("arbitrary", "arbitrary", "arbitrary"),
        name=f"matmul_res_{tm}_{k}",
    )(a, w, res)


def _rel_bucket(dist):
    max_exact = N_BUCKETS // 2
    d32 = jnp.maximum(dist, 1).astype(F32)
    large = max_exact + (jnp.log(d32 / max_exact) / math.log(MAX_DISTANCE / max_exact)
                         * (N_BUCKETS - max_exact)).astype(jnp.int32)
    large = jnp.minimum(large, N_BUCKETS - 1)
    return jnp.where(dist < max_exact, dist, large)


def _bias_by_stride(bias_table):
    m = jnp.arange(BAND + 1)
    return jnp.stack([bias_table[:, g][_rel_bucket(m * d)] for g, (_, d) in enumerate(DILATED_PAIRS)])


def _prompt_bias(bias_m):
    a = jnp.arange(BAND)[:, None]
    b = jnp.arange(2 * BAND)[None, :]
    m = BAND + a - b
    valid = (m >= 0) & (m <= BAND)
    t = bias_m[:, jnp.clip(m, 0, BAND)]
    t = jnp.where(valid[None, :, :, None], t, MASKED)
    return t.transpose(0, 3, 1, 2)


def _attn_a_prompt_body(q0, k0, v0, q1, k1, v1, q2, k2, v2, bias_ref, o_ref, og_ref, lse_ref, *, t):
    qkv = ((q0, k0, v0), (q1, k1, v1), (q2, k2, v2))

    def rows_of(start, d):
        if d == 1:
            return pl.ds(start, BAND)
        return pl.ds(start, BAND, stride=d)

    def block(g, d, cur, prev):
        q_ref, k_ref, v_ref = qkv[g]
        q = q_ref[cur, :].astype(BF16)
        kc = k_ref[cur, :].astype(BF16)
        vc = v_ref[cur, :].astype(BF16)
        dn = (((1,), (1,)), ((), ()))
        s_c = lax.dot_general(q, kc, dn, preferred_element_type=F32) * ATT_SCALE + bias_ref[g, :, BAND:]
        mx = jnp.max(s_c, axis=-1, keepdims=True)
        if prev is not None:
            kp = k_ref[prev, :].astype(BF16)
            vp = v_ref[prev, :].astype(BF16)
            s_p = lax.dot_general(q, kp, dn, preferred_element_type=F32) * ATT_SCALE + bias_ref[g, :, :BAND]
            mx = jnp.maximum(mx, jnp.max(s_p, axis=-1, keepdims=True))
        p_c = jnp.exp(s_c - mx)
        den = jnp.sum(p_c, axis=-1, keepdims=True)
        acc = jnp.dot(p_c.astype(BF16), vc, preferred_element_type=F32)
        if prev is not None:
            p_p = jnp.exp(s_p - mx)
            den = den + jnp.sum(p_p, axis=-1, keepdims=True)
            acc = acc + jnp.dot(p_p.astype(BF16), vp, preferred_element_type=F32)
        og_ref[g, cur, :] = acc / den
        lse_ref[g, cur, :] = jnp.broadcast_to(mx + jnp.log(den), (BAND, HEAD_DIM))

    for g, (_, d) in enumerate(DILATED_PAIRS):
        span = BAND * d
        n_blk = t // span
        if n_blk == 1:
            def residue_step(r, carry, g=g, d=d):
                block(g, d, rows_of(r, d), None)
                return carry
            lax.fori_loop(0, d, residue_step, 0)
            continue
        for r in range(d):
            block(g, d, rows_of(r, d), None)

            def blk_step(n, carry, g=g, d=d, r=r, span=span):
                start = n * span + r
                if d == 1:
                    start = pl.multiple_of(start, BAND)
                block(g, d, rows_of(start, d), rows_of(start - span, d))
                return carry
            lax.fori_loop(1, n_blk, blk_step, 0)

    def merge_step(c, carry):
        rows = pl.ds(pl.multiple_of(c * BAND, BAND), BAND)
        l0, l1, l2 = lse_ref[0, rows, :], lse_ref[1, rows, :], lse_ref[2, rows, :]
        mx = jnp.maximum(jnp.maximum(l0, l1), l2)
        e0, e1, e2 = jnp.exp(l0 - mx), jnp.exp(l1 - mx), jnp.exp(l2 - mx)
        tot = e0 + e1 + e2
        o = (e0 / tot) * og_ref[0, rows, :] + (e1 / tot) * og_ref[1, rows, :] + (e2 / tot) * og_ref[2, rows, :]
        o_ref[rows, :] = o.astype(o_ref.dtype)
        return carry

    lax.fori_loop(0, t // BAND, merge_step, 0)


def _attn_a_prompt(qkv, bias, *, b, t):
    heads = N_A_HEADS

    def col_spec(g, c):
        base = (g * 3 + c) * heads
        return pl.BlockSpec((t, HEAD_DIM), lambda bi, h: (bi, base + h))

    in_specs = [col_spec(g, c) for g in range(N_GROUPS) for c in range(3)]
    in_specs.append(pl.BlockSpec((N_GROUPS, None, BAND, 2 * BAND), lambda bi, h: (0, h, 0, 0)))
    return pl.pallas_call(
        functools.partial(_attn_a_prompt_body, t=t),
        grid=(b, heads),
        in_specs=in_specs,
        out_specs=pl.BlockSpec((t, HEAD_DIM), lambda bi, h: (bi, h)),
        out_shape=jax.ShapeDtypeStruct((b * t, A_WIDTH), BF16),
        scratch_shapes=[pltpu.VMEM((N_GROUPS, t, HEAD_DIM), F32),
                        pltpu.VMEM((N_GROUPS, t, HEAD_DIM), F32)],
        compiler_params=_params("arbitrary", "arbitrary"),
        name="attn_a_prompt",
    )(*([qkv] * 9), bias)


def _attn_a_sample_body(qkv_ref, c0_ref, c1_ref, c2_ref, bb0_ref, bb12_ref, bn0_ref, bn12_ref, o_ref, *, s_new):
    s = pl.program_id(1)
    outs, lses = [], []
    for g in range(N_GROUPS):
        q = qkv_ref[s, g, 0]
        if g == 0:
            kb, vb, bias_b = c0_ref[:, 0], c0_ref[:, 1], bb0_ref[...]
            kn, vn, bias_n = qkv_ref[:, 0, 1], qkv_ref[:, 0, 2], bn0_ref[...]
        else:
            c_ref = c1_ref if g == 1 else c2_ref
            kb, vb, bias_b = c_ref[:, 0], c_ref[:, 1], bb12_ref[g - 1]
            kn, vn, bias_n = qkv_ref[pl.ds(s, 1), g, 1], qkv_ref[pl.ds(s, 1), g, 2], bn12_ref[g - 1][None]
        lb = jnp.sum(q[None] * kb, axis=-1, keepdims=True) * ATT_SCALE + bias_b
        ln = jnp.sum(q[None] * kn, axis=-1, keepdims=True) * ATT_SCALE + bias_n
        mx = jnp.maximum(jnp.max(lb, axis=0), jnp.max(ln, axis=0))
        pb = jnp.exp(lb - mx[None])
        pn = jnp.exp(ln - mx[None])
        den = jnp.sum(pb, axis=0) + jnp.sum(pn, axis=0)
        acc = jnp.sum(pb * vb, axis=0) + jnp.sum(pn * vn, axis=0)
        outs.append(acc / den)
        lses.append(mx + jnp.log(den))
    mx = jnp.maximum(jnp.maximum(lses[0], lses[1]), lses[2])
    es = [jnp.exp(l - mx) for l in lses]
    tot = es[0] + es[1] + es[2]
    o_ref[...] = (es[0] / tot) * outs[0] + (es[1] / tot) * outs[1] + (es[2] / tot) * outs[2]


def _attn_a_sample(qkv_s, caches, bias_m, *, bs, s_new):
    h = N_A_HEADS
    lane = (h, HEAD_DIM)

    def dense(x):
        return jnp.broadcast_to(x[..., None], x.shape + (HEAD_DIM,)).astype(F32)

    sq = jnp.arange(s_new)
    r = jnp.arange(BAND)
    m_b = BAND + sq[:, None] - r[None, :]
    bb0 = dense(jnp.where((m_b <= BAND)[..., None], bias_m[0][jnp.clip(m_b, 0, BAND)], MASKED))
    m_n = sq[:, None] - sq[None, :]
    bn0 = dense(jnp.where((m_n >= 0)[..., None], bias_m[0][jnp.clip(m_n, 0, BAND)], MASKED))
    bb12 = dense(jnp.stack([bias_m[g][BAND - r] for g in (1, 2)]))
    bn12 = dense(jnp.stack([bias_m[g][0] for g in (1, 2)]))

    c0 = caches[0]
    c1 = caches[1].reshape(bs, BAND, DILATED_PAIRS[1][1], 2, h, HEAD_DIM)
    c2 = caches[2].reshape(bs, BAND, DILATED_PAIRS[2][1], 2, h, HEAD_DIM)
    return pl.pallas_call(
        functools.partial(_attn_a_sample_body, s_new=s_new),
        grid=(bs, s_new),
        in_specs=[
            pl.BlockSpec((None, s_new, N_GROUPS, 3) + lane, lambda b, s: (b, 0, 0, 0, 0, 0)),
            pl.BlockSpec((None, BAND, 2) + lane, lambda b, s: (b, 0, 0, 0, 0)),
            pl.BlockSpec((None, BAND, None, 2) + lane, lambda b, s: (b, 0, s, 0, 0, 0)),
            pl.BlockSpec((None, BAND, None, 2) + lane, lambda b, s: (b, 0, s, 0, 0, 0)),
            pl.BlockSpec((None, BAND) + lane, lambda b, s: (s, 0, 0, 0)),
            pl.BlockSpec((2, BAND) + lane, lambda b, s: (0, 0, 0, 0)),
            pl.BlockSpec((None, s_new) + lane, lambda b, s: (s, 0, 0, 0)),
            pl.BlockSpec((2,) + lane, lambda b, s: (0, 0, 0)),
        ],
        out_specs=pl.BlockSpec((None, None) + lane, lambda b, s: (b, s, 0, 0)),
        out_shape=jax.ShapeDtypeStruct((bs, s_new) + lane, F32),
        compiler_params=_params("arbitrary", "arbitrary"),
        name="attn_a_sample",
    )(qkv_s, c0, c1, c2, bb0, bb12, bn0, bn12)


def _softplus(z):
    return jnp.maximum(z, 0.0) + jnp.log1p(jnp.exp(-jnp.abs(z)))


def _later_sums(l, tri_ones):
    hi = l.astype(BF16)
    lo = (l - hi.astype(F32)).astype(BF16)
    s = jnp.dot(hi, tri_ones, preferred_element_type=F32) + jnp.dot(lo, tri_ones, preferred_element_type=F32)
    return s[:, :PAGE_SIZE], s[:, PAGE_SIZE:]


def _tri_ones():
    j = lax.broadcasted_iota(jnp.int32, (PAGE_SIZE, 2 * PAGE_SIZE), 0)
    c = lax.broadcasted_iota(jnp.int32, (PAGE_SIZE, 2 * PAGE_SIZE), 1)
    return jnp.where((c >= PAGE_SIZE) | (j > c), 1.0, 0.0).astype(BF16)


def _attn_b_prompt_body(q_ref, k_ref, v_ref, bias_ref, o_ref, kb_ref, vb_ref, *, t, tq):
    i = pl.program_id(2)

    @pl.when(i == 0)
    def _():
        def cast_step(c, carry):
            rows = pl.ds(pl.multiple_of(c * 256, 256), 256)
            kb_ref[rows, :] = k_ref[rows, :].astype(BF16)
            vb_ref[rows, :] = v_ref[rows, :].astype(BF16)
            return carry
        lax.fori_loop(0, t // 256, cast_step, 0)

    rows = GQ * tq
    q = jnp.concatenate([q_ref[:, g * HEAD_DIM:(g + 1) * HEAD_DIM] for g in range(GQ)], axis=0)
    bias = bias_ref[...]
    tri_ones = _tri_ones()
    qpos = lax.broadcasted_iota(jnp.int32, (rows, PAGE_SIZE), 0) % tq
    kpos = lax.broadcasted_iota(jnp.int32, (rows, PAGE_SIZE), 1)
    below_diag = kpos < qpos

    def kv_step(jj, carry):
        run, acc = carry
        jblk = i - jj
        krows = pl.ds(pl.multiple_of(jblk * PAGE_SIZE, PAGE_SIZE), PAGE_SIZE)
        k = kb_ref[krows, :]
        v = vb_ref[krows, :]
        z = lax.dot_general(q, k, (((1,), (1,)), ((), ())), preferred_element_type=F32) * ATT_SCALE + bias
        valid = below_diag | (jj > 0)
        sp = _softplus(z)
        l = jnp.where(valid, -sp, 0.0)
        later, total = _later_sums(l, tri_ones)
        a = jnp.where(valid, jnp.exp(z - sp + later + run), 0.0)
        acc = acc + jnp.dot(a.astype(BF16), v, preferred_element_type=F32)
        return run + total, acc

    zeros = jnp.zeros((rows, HEAD_DIM), F32)
    _, acc = lax.fori_loop(0, i + 1, kv_step, (zeros, zeros))
    for g in range(GQ):
        o_ref[:, g * HEAD_DIM:(g + 1) * HEAD_DIM] = acc[g * tq:(g + 1) * tq].astype(o_ref.dtype)


def _attn_b_prompt(q, kv, bias_rows, *, b, t):
    tq = PAGE_SIZE
    nq = t // tq
    return pl.pallas_call(
        functools.partial(_attn_b_prompt_body, t=t, tq=tq),
        grid=(b, N_KV_HEADS, nq),
        in_specs=[
            pl.BlockSpec((tq, GQ * HEAD_DIM), lambda bi, kh, i: (bi * nq + i, kh)),
            pl.BlockSpec((t, HEAD_DIM), lambda bi, kh, i: (bi, kh)),
            pl.BlockSpec((t, HEAD_DIM), lambda bi, kh, i: (bi, N_KV_HEADS + kh)),
            pl.BlockSpec((None, GQ * tq, HEAD_DIM), lambda bi, kh, i: (kh, 0, 0)),
        ],
        out_specs=pl.BlockSpec((tq, GQ * HEAD_DIM), lambda bi, kh, i: (bi * nq + i, kh)),
        out_shape=jax.ShapeDtypeStruct((b * t, N_B_HEADS * HEAD_DIM), BF16),
        scratch_shapes=[pltpu.VMEM((t, HEAD_DIM), BF16), pltpu.VMEM((t, HEAD_DIM), BF16)],
        compiler_params=_params("arbitrary", "arbitrary", "arbitrary"),
        name="attn_b_prompt",
    )(q, kv, kv, bias_rows)


def _attn_b_sample_body(pt_ref, q_ref, kvn_ref, bias_ref, page_ref, o_ref, run_ref, acc_ref, *, s_new, n_pages):
    p = pl.program_id(1)
    rows = GQ * s_new
    row_tok = lax.broadcasted_iota(jnp.int32, (rows, HEAD_DIM), 0) % s_new

    @pl.when(p == 0)
    def _():
        for kh in range(N_KV_HEADS):
            q = q_ref[kh]
            run = jnp.zeros((rows, HEAD_DIM), F32)
            acc = jnp.zeros((rows, HEAD_DIM), F32)
            for j in reversed(range(s_new)):
                kj = kvn_ref[0, kh, pl.ds(j, 1), :]
                vj = kvn_ref[1, kh, pl.ds(j, 1), :]
                z = jnp.sum(q * kj, axis=-1, keepdims=True) * ATT_SCALE + bias_ref[kh]
                valid = row_tok > j
                sp = _softplus(z)
                a = jnp.where(valid, jnp.exp(z - sp + run), 0.0)
                acc = acc + a * vj
                run = run + jnp.where(valid, -sp, 0.0)
            run_ref[kh] = run
            acc_ref[kh] = acc

    tri_ones = _tri_ones()
    stride = 2 * N_KV_HEADS
    for kh in range(N_KV_HEADS):
        k = page_ref[pl.ds(kh, PAGE_SIZE, stride=stride), :].astype(BF16)
        v = page_ref[pl.ds(N_KV_HEADS + kh, PAGE_SIZE, stride=stride), :].astype(BF16)
        q = q_ref[kh].astype(BF16)
        z = lax.dot_general(q, k, (((1,), (1,)), ((), ())), preferred_element_type=F32) * ATT_SCALE + bias_ref[kh]
        sp = _softplus(z)
        later, total = _later_sums(-sp, tri_ones)
        a = jnp.exp(z - sp + later + run_ref[kh])
        acc_ref[kh] += jnp.dot(a.astype(BF16), v, preferred_element_type=F32)
        run_ref[kh] += total

    @pl.when(p == n_pages - 1)
    def _():
        o_ref[...] = acc_ref[...]


def _attn_b_sample(q_rows, kv_new, bias_rows, cache_kv, page_table, *, s_new):
    bs, n_pages = page_table.shape
    rows = GQ * s_new
    pool = cache_kv.reshape(cache_kv.shape[0], PAGE_SIZE * 2 * N_KV_HEADS, HEAD_DIM)
    grid_spec = pltpu.PrefetchScalarGridSpec(
        num_scalar_prefetch=1,
        grid=(bs, n_pages),
        in_specs=[
            pl.BlockSpec((None, N_KV_HEADS, rows, HEAD_DIM), lambda b, p, pt: (b, 0, 0, 0)),
            pl.BlockSpec((None, 2, N_KV_HEADS, s_new, HEAD_DIM), lambda b, p, pt: (b, 0, 0, 0, 0)),
            pl.BlockSpec((N_KV_HEADS, rows, HEAD_DIM), lambda b, p, pt: (0, 0, 0)),
            pl.BlockSpec((None, PAGE_SIZE * 2 * N_KV_HEADS, HEAD_DIM),
                         lambda b, p, pt: (pt[b, n_pages - 1 - p], 0, 0)),
        ],
        out_specs=pl.BlockSpec((None, N_KV_HEADS, rows, HEAD_DIM), lambda b, p, pt: (b, 0, 0, 0)),
        scratch_shapes=[pltpu.VMEM((N_KV_HEADS, rows, HEAD_DIM), F32),
                        pltpu.VMEM((N_KV_HEADS, rows, HEAD_DIM), F32)],
    )
    return pl.pallas_call(
        functools.partial(_attn_b_sample_body, s_new=s_new, n_pages=n_pages),
        grid_spec=grid_spec,
        out_shape=jax.ShapeDtypeStruct((bs, N_KV_HEADS, rows, HEAD_DIM), F32),
        compiler_params=_params("arbitrary", "arbitrary"),
        name="attn_b_sample",
    )(page_table, q_rows, kv_new, bias_rows, pool)


def _mlp(x, gain, w_up, w_down, *, tm):
    tn = 512 if tm >= 512 else 1024
    h = _norm_matmul(x, gain, w_up, mode="relu2", out_dtype=BF16, tm=tm, tn=tn)
    return _matmul_res(h, w_down, x, tm=tm, tn=1024, tk=2048)


@jax.jit
def _step(x_prompt, x_sample, cache_win_g0, cache_win_g1, cache_win_g2, cache_kv, page_table, bias_table,
          norm_mix, norm_mlp, w_a_in, g_a_q, g_a_k, w_a_out, norm_kv, w_kv, w_b_q, b_sb, w_b_out, w_up, w_down):
    bp, tp, _ = x_prompt.shape
    bs, ts, _ = x_sample.shape
    tm_p, tm_s = 1024, bs * ts
    tn_p = 512
    xp = x_prompt.reshape(bp * tp, D_MODEL)
    xs = x_sample.reshape(bs * ts, D_MODEL)
    caches = (cache_win_g0[0], cache_win_g1[0], cache_win_g2[0])

    w_in = w_a_in[0].astype(BF16)
    w_ao = w_a_out[0].astype(BF16)
    w_kvb = w_kv.astype(BF16)
    w_bq = w_b_q[0].astype(BF16)
    w_bo = w_b_out[0].astype(BF16)
    w_upb = w_up.astype(BF16)
    w_dnb = w_down.astype(BF16)

    head_gain = jnp.concatenate([jnp.tile(g_a_q[0], N_A_HEADS), jnp.tile(g_a_k[0], N_A_HEADS),
                                 jnp.ones((A_WIDTH,), F32)])
    col_gain = jnp.tile(head_gain, N_GROUPS).reshape(1, -1).astype(F32)
    qkv_p = _norm_matmul(xp, norm_mix[0], w_in, mode="qkv", out_dtype=F32, tm=tm_p, tn=tn_p, col_gain=col_gain)
    qkv_s = _norm_matmul(xs, norm_mix[0], w_in, mode="qkv", out_dtype=F32, tm=tm_s, tn=1024, col_gain=col_gain)

    bias_m = _bias_by_stride(bias_table)
    o_p = _attn_a_prompt(qkv_p, _prompt_bias(bias_m), b=bp, t=tp)
    qkv_s6 = qkv_s.reshape(bs, ts, N_GROUPS, 3, N_A_HEADS, HEAD_DIM)
    o_s = _attn_a_sample(qkv_s6, caches, bias_m, bs=bs, s_new=ts)
    o_s = o_s.reshape(bs * ts, A_WIDTH).astype(BF16)

    xp = _matmul_res(o_p, w_ao, xp, tm=tm_p, tn=1024, tk=2048)
    xs = _matmul_res(o_s, w_ao, xs, tm=tm_s, tn=1024, tk=2048)
    xp = _mlp(xp, norm_mlp[0], w_upb[0], w_dnb[0], tm=tm_p)
    xs = _mlp(xs, norm_mlp[0], w_upb[0], w_dnb[0], tm=tm_s)

    kv_p = _norm_matmul(xp, norm_kv, w_kvb, mode="plain", out_dtype=F32, tm=tm_p, tn=tn_p)
    kv_s = _norm_matmul(xs, norm_kv, w_kvb, mode="plain", out_dtype=F32, tm=tm_s, tn=1024)

    qb_p = _norm_matmul(xp, norm_mix[1], w_bq, mode="plain", out_dtype=BF16, tm=tm_p, tn=tn_p)
    qb_s = _norm_matmul(xs, norm_mix[1], w_bq, mode="plain", out_dtype=F32, tm=tm_s, tn=1024)

    sb = b_sb[0].astype(F32).reshape(N_KV_HEADS, GQ)

    def bias_rows(per):
        return jnp.broadcast_to(sb[:, :, None, None], (N_KV_HEADS, GQ, per, HEAD_DIM)).reshape(
            N_KV_HEADS, GQ * per, HEAD_DIM)

    ob_p = _attn_b_prompt(qb_p, kv_p, bias_rows(PAGE_SIZE), b=bp, t=tp)
    q_rows = qb_s.reshape(bs, ts, N_KV_HEADS, GQ, HEAD_DIM).transpose(0, 2, 3, 1, 4).reshape(
        bs, N_KV_HEADS, GQ * ts, HEAD_DIM)
    kv_s5 = kv_s.reshape(bs, ts, 2, N_KV_HEADS, HEAD_DIM)
    ob_s = _attn_b_sample(q_rows, kv_s5.transpose(0, 2, 3, 1, 4), bias_rows(ts), cache_kv, page_table, s_new=ts)
    ob_s = ob_s.reshape(bs, N_KV_HEADS, GQ, ts, HEAD_DIM).transpose(0, 3, 1, 2, 4).reshape(
        bs * ts, N_B_HEADS * HEAD_DIM).astype(BF16)

    xp = _matmul_res(ob_p, w_bo, xp, tm=tm_p, tn=1024, tk=2048)
    xs = _matmul_res(ob_s, w_bo, xs, tm=tm_s, tn=1024, tk=2048)
    xp = _mlp(xp, norm_mlp[1], w_upb[1], w_dnb[1], tm=tm_p)
    xs = _mlp(xs, norm_mlp[1], w_upb[1], w_dnb[1], tm=tm_s)

    qkv_p6 = qkv_p.reshape(bp, tp, N_GROUPS, 3, N_A_HEADS, HEAD_DIM)
    win_p, win_s = [], []
    for g, (window, _) in enumerate(DILATED_PAIRS):
        wp = min(window, tp)
        win_p.append(qkv_p6[:, tp - wp:, g, 1:3][None])
        buf = caches[g]
        win_s.append(jnp.concatenate([buf[:, ts:], qkv_s6[:, :, g, 1:3]], axis=1)[None])
    return (xp.reshape(bp, tp, D_MODEL), xs.reshape(bs, ts, D_MODEL), *win_p, *win_s,
            kv_p.reshape(bp, tp, 2, N_KV_HEADS, HEAD_DIM), kv_s5)


def kernel(x_prompt, x_sample, cache_win_g0, cache_win_g1, cache_win_g2, cache_kv, page_table, bias_table,
           norm_mix, norm_mlp, w_a_in, g_a_q, g_a_k, w_a_out, norm_kv, w_kv, w_b_q, b_sb, w_b_out, w_up, w_down):
    return _step(x_prompt, x_sample, cache_win_g0, cache_win_g1, cache_win_g2, cache_kv, page_table, bias_table,
                 norm_mix, norm_mlp, w_a_in, g_a_q, g_a_k, w_a_out, norm_kv, w_kv, w_b_q, b_sb, w_b_out,
                 w_up, w_down)
```

```python
import functools
import math
from typing import NamedTuple

import jax
import jax.numpy as jnp
from jax import lax
from jax.experimental import pallas as pl
from jax.experimental.pallas import tpu as pltpu

F32 = jnp.float32
BF16 = jnp.bfloat16

D_MODEL = 4096
HEAD_DIM = 128
DILATED_PAIRS = ((128, 1), (512, 4), (2048, 16))
N_GROUPS = len(DILATED_PAIRS)
N_A_HEADS = D_MODEL // (2 * HEAD_DIM)
A_WIDTH = N_A_HEADS * HEAD_DIM
N_B_HEADS = D_MODEL // HEAD_DIM
N_KV_HEADS = 8
GQ = N_B_HEADS // N_KV_HEADS
N_BUCKETS = 32
MAX_DISTANCE = 2048
RMS_EPS = 1e-6
PAGE_SIZE = 128
BAND = 128
ATT_SCALE = HEAD_DIM ** -0.5
MASKED = -1e30

V7X_VMEM_LIMIT_BYTES = 56 * 1024 * 1024
V7X_MXU_COLS = 256
NORM_ROWS = 16
PAGES_PER_STEP = 4


class Tiles(NamedTuple):
    tm: int
    tn_full_k: int
    tn_res: int
    tk_res: int


PROMPT_TILES = Tiles(tm=1024, tn_full_k=512, tn_res=1024, tk_res=2048)


def _sample_tiles(rows):
    return Tiles(tm=rows, tn_full_k=512, tn_res=1024, tk_res=2048)


def _params(*semantics):
    return pltpu.CompilerParams(dimension_semantics=semantics,
                                vmem_limit_bytes=V7X_VMEM_LIMIT_BYTES)


def _weight_spec(w, layer, block, index):
    if w.ndim == 2:
        return pl.BlockSpec(block, index)
    return pl.BlockSpec((None,) + block, lambda *g: (layer,) + index(*g))


def _head_rms(t, cg):
    ms = jnp.mean(t * t, axis=-1, keepdims=True)
    return t * lax.rsqrt(ms + RMS_EPS) * cg


def _norm_matmul_body(x_ref, g_ref, w_ref, cg_ref, o_ref, *rest, mode, tm, tn, emit_w):
    if emit_w:
        wb_ref, xn_ref = rest
    else:
        (xn_ref,) = rest
    j = pl.program_id(1)

    @pl.when(j == 0)
    def _():
        g = g_ref[...]

        def rows_step(c, carry):
            rows = pl.ds(pl.multiple_of(c * NORM_ROWS, NORM_ROWS), NORM_ROWS)
            x = x_ref[rows, :]
            ms = jnp.mean(x * x, axis=-1, keepdims=True)
            xn_ref[rows, :] = (x * lax.rsqrt(ms + RMS_EPS) * g).astype(BF16)
            return carry

        lax.fori_loop(0, tm // NORM_ROWS, rows_step, 0)

    if emit_w:
        wb_ref[...] = w_ref[...].astype(BF16)
        w_src = wb_ref
    else:
        w_src = w_ref

    if mode == "relu2":
        r = jnp.maximum(jnp.dot(xn_ref[...], w_src[...], preferred_element_type=F32), 0.0)
        o_ref[...] = (r * r).astype(o_ref.dtype)
    elif mode == "plain":
        o_ref[...] = jnp.dot(xn_ref[...], w_src[...], preferred_element_type=F32).astype(o_ref.dtype)
    else:
        kind = (j * tn // A_WIDTH) % 3

        @pl.when(kind == 2)
        def _():
            o_ref[...] = jnp.dot(xn_ref[...], w_src[...], preferred_element_type=F32)

        @pl.when(kind != 2)
        def _():
            for c0 in range(0, tn, V7X_MXU_COLS):
                y = jnp.dot(xn_ref[...], w_src[:, c0:c0 + V7X_MXU_COLS], preferred_element_type=F32)
                for c in range(0, V7X_MXU_COLS, HEAD_DIM):
                    cols = slice(c0 + c, c0 + c + HEAD_DIM)
                    o_ref[:, cols] = _head_rms(y[:, c:c + HEAD_DIM], cg_ref[:, cols])


def _norm_matmul(x, gain, w, *, mode, out_dtype, tiles, layer=0, col_gain=None, emit_w=False):
    m, k = x.shape
    n = w.shape[-1]
    tm, tn = tiles.tm, tiles.tn_full_k
    assert not emit_w or m == tm
    if col_gain is None:
        col_gain = jnp.ones((1, n), F32)
    body = functools.partial(_norm_matmul_body, mode=mode, tm=tm, tn=tn, emit_w=emit_w)
    out_specs = pl.BlockSpec((tm, tn), lambda i, j: (i, j))
    out_shape = jax.ShapeDtypeStruct((m, n), out_dtype)
    if emit_w:
        out_specs = [out_specs, pl.BlockSpec((k, tn), lambda i, j: (0, j))]
        out_shape = [out_shape, jax.ShapeDtypeStruct((k, n), BF16)]
    return pl.pallas_call(
        body,
        grid=(m // tm, n // tn),
        in_specs=[
            pl.BlockSpec((tm, k), lambda i, j: (i, 0), pipeline_mode=pl.Buffered(1)),
            pl.BlockSpec((1, k), lambda i, j: (0, 0)),
            _weight_spec(w, layer, (k, tn), lambda i, j: (0, j)),
            pl.BlockSpec((1, tn), lambda i, j: (0, j)),
        ],
        out_specs=out_specs,
        out_shape=out_shape,
        scratch_shapes=[pltpu.VMEM((tm, k), BF16)],
        compiler_params=_params("arbitrary", "arbitrary"),
        name=f"norm_matmul_{mode}_{tm}",
    )(x, gain.reshape(1, k), w, col_gain)


def _matmul_res_body(a_ref, w_ref, r_ref, o_ref, *rest, emit_w):
    kk = pl.program_id(2)
    if emit_w:
        (wb_ref,) = rest
        wb_ref[...] = w_ref[...].astype(BF16)
        w_src = wb_ref
    else:
        w_src = w_ref
    d = jnp.dot(a_ref[...], w_src[...], preferred_element_type=F32)

    @pl.when(kk == 0)
    def _():
        o_ref[...] = r_ref[...] + d

    @pl.when(kk != 0)
    def _():
        o_ref[...] += d


def _matmul_res(a, w, res, *, tiles, layer=0, emit_w=False):
    m, k = a.shape
    n = w.shape[-1]
    tm, tn, tk = tiles.tm, tiles.tn_res, tiles.tk_res
    assert not emit_w or m == tm
    out_specs = pl.BlockSpec((tm, tn), lambda i, j, kk: (i, j))
    out_shape = jax.ShapeDtypeStruct((m, n), F32)
    if emit_w:
        out_specs = [out_specs, pl.BlockSpec((tk, tn), lambda i, j, kk: (kk, j))]
        out_shape = [out_shape, jax.ShapeDtypeStruct((k, n), BF16)]
    return pl.pallas_call(
        functools.partial(_matmul_res_body, emit_w=emit_w),
        grid=(m // tm, n // tn, k // tk),
        in_specs=[
            pl.BlockSpec((tm, tk), lambda i, j, kk: (i, kk)),
            _weight_spec(w, layer, (tk, tn), lambda i, j, kk: (kk, j)),
            pl.BlockSpec((tm, tn), lambda i, j, kk: (i, j)),
        ],
        out_specs=out_specs,
        out_shape=out_shape,
        compiler_params=_params("arbitrary", "arbitrary", "arbitrary"),
        name=f"matmul_res_{tm}_{k}",
    )(a, w, res)


def _rel_bucket(dist):
    max_exact = N_BUCKETS // 2
    d32 = jnp.maximum(dist, 1).astype(F32)
    large = max_exact + (jnp.log(d32 / max_exact) / math.log(MAX_DISTANCE / max_exact)
                         * (N_BUCKETS - max_exact)).astype(jnp.int32)
    large = jnp.minimum(large, N_BUCKETS - 1)
    return jnp.where(dist < max_exact, dist, large)


def _bias_by_stride(bias_table):
    m = jnp.arange(BAND + 1)
    return jnp.stack([bias_table[:, g][_rel_bucket(m * d)] for g, (_, d) in enumerate(DILATED_PAIRS)])


def _prompt_bias(bias_m):
    period = 2 * BAND + 1
    u = jnp.concatenate([bias_m[:, ::-1, :], jnp.full((N_GROUPS, period - BAND - 1, N_A_HEADS), MASKED, F32)],
                        axis=1).transpose(0, 2, 1)
    flat = jnp.tile(u, (1, 1, BAND))[:, :, :BAND * 2 * BAND]
    return flat.reshape(N_GROUPS, N_A_HEADS, BAND, 2 * BAND)


def _attn_a_prompt_body(q0, k0, v0, q1, k1, v1, q2, k2, v2, bias_ref, o_ref, og_ref, lse_ref, *, t):
    qkv = ((q0, k0, v0), (q1, k1, v1), (q2, k2, v2))
    ones = jnp.ones((BAND, HEAD_DIM), BF16)
    dn = (((1,), (1,)), ((), ()))

    def rows_of(start, d):
        if d == 1:
            return pl.ds(start, BAND)
        return pl.ds(start, BAND, stride=d)

    def blocks(g, specs):
        q_ref, k_ref, v_ref = qkv[g]
        with_prev = specs[0][1] is not None

        def logits(q, rows, bias):
            return lax.dot_general(q, k_ref[rows, :].astype(BF16), dn, preferred_element_type=F32) * ATT_SCALE + bias

        def weighted(p, rows):
            vx = jnp.concatenate([v_ref[rows, :].astype(BF16), ones], axis=1)
            return jnp.dot(p.astype(BF16), vx, preferred_element_type=F32)

        qs = [q_ref[cur, :].astype(BF16) for cur, _ in specs]
        s_cs = [logits(q, cur, bias_ref[g, :, BAND:]) for q, (cur, _) in zip(qs, specs)]
        mxs = [jnp.max(s, axis=-1, keepdims=True) for s in s_cs]
        if with_prev:
            s_ps = [logits(q, prev, bias_ref[g, :, :BAND]) for q, (_, prev) in zip(qs, specs)]
            mxs = [jnp.maximum(mx, jnp.max(s, axis=-1, keepdims=True)) for mx, s in zip(mxs, s_ps)]
        accs = [weighted(jnp.exp(s - mx), cur) for s, mx, (cur, _) in zip(s_cs, mxs, specs)]
        if with_prev:
            accs = [acc + weighted(jnp.exp(s - mx), prev) for acc, s, mx, (_, prev) in zip(accs, s_ps, mxs, specs)]
        for acc, mx, (cur, _) in zip(accs, mxs, specs):
            den = acc[:, HEAD_DIM:]
            og_ref[g, cur, :] = acc[:, :HEAD_DIM] / den
            lse_ref[g, cur, :] = mx + jnp.log(den)

    for g, (_, d) in enumerate(DILATED_PAIRS):
        span = BAND * d
        n_blk = t // span
        if n_blk == 1:
            per = 4

            def residue_step(it, carry, g=g, d=d, per=per):
                blocks(g, [(rows_of(it * per + u, d), None) for u in range(per)])
                return carry
            lax.fori_loop(0, d // per, residue_step, 0)
        elif d == 1:
            per = 5
            assert (n_blk - 1) % per == 0
            blocks(g, [(rows_of(0, d), None)])

            def blk_step(it, carry, g=g, d=d, per=per, span=span):
                starts = [pl.multiple_of((1 + it * per + u) * span, BAND) for u in range(per)]
                blocks(g, [(rows_of(st, d), rows_of(st - span, d)) for st in starts])
                return carry
            lax.fori_loop(0, (n_blk - 1) // per, blk_step, 0)
        else:
            blocks(g, [(rows_of(r, d), None) for r in range(d)])

            def blk_step(n, carry, g=g, d=d, span=span):
                blocks(g, [(rows_of(n * span + r, d), rows_of(n * span + r - span, d)) for r in range(d)])
                return carry
            lax.fori_loop(1, n_blk, blk_step, 0)

    def merge_step(c, carry):
        rows = pl.ds(pl.multiple_of(c * BAND, BAND), BAND)
        l0, l1, l2 = lse_ref[0, rows, :], lse_ref[1, rows, :], lse_ref[2, rows, :]
        mx = jnp.maximum(jnp.maximum(l0, l1), l2)
        e0, e1, e2 = jnp.exp(l0 - mx), jnp.exp(l1 - mx), jnp.exp(l2 - mx)
        tot = e0 + e1 + e2
        o = (e0 / tot) * og_ref[0, rows, :] + (e1 / tot) * og_ref[1, rows, :] + (e2 / tot) * og_ref[2, rows, :]
        o_ref[rows, :] = o.astype(o_ref.dtype)
        return carry

    lax.fori_loop(0, t // BAND, merge_step, 0)


def _attn_a_prompt(qkv, bias, *, b, t):
    heads = N_A_HEADS

    def col_spec(g, c):
        base = (g * 3 + c) * heads
        return pl.BlockSpec((t, HEAD_DIM), lambda bi, h: (bi, base + h))

    in_specs = [col_spec(g, c) for g in range(N_GROUPS) for c in range(3)]
    in_specs.append(pl.BlockSpec((N_GROUPS, None, BAND, 2 * BAND), lambda bi, h: (0, h, 0, 0)))
    return pl.pallas_call(
        functools.partial(_attn_a_prompt_body, t=t),
        grid=(b, heads),
        in_specs=in_specs,
        out_specs=pl.BlockSpec((t, HEAD_DIM), lambda bi, h: (bi, h)),
        out_shape=jax.ShapeDtypeStruct((b * t, A_WIDTH), BF16),
        scratch_shapes=[pltpu.VMEM((N_GROUPS, t, HEAD_DIM), F32),
                        pltpu.VMEM((N_GROUPS, t, HEAD_DIM), F32)],
        compiler_params=_params("arbitrary", "arbitrary"),
        name="attn_a_prompt",
    )(*([qkv] * 9), bias)


def _attn_a_sample_body(qkv_ref, c0_ref, c1_ref, c2_ref, bb0_ref, bb12_ref, bn0_ref, bn12_ref, o_ref):
    s = pl.program_id(1)
    outs, lses = [], []
    for g in range(N_GROUPS):
        q = qkv_ref[s, g, 0]
        if g == 0:
            kb, vb, bias_b = c0_ref[:, 0], c0_ref[:, 1], bb0_ref[...]
            kn, vn, bias_n = qkv_ref[:, 0, 1], qkv_ref[:, 0, 2], bn0_ref[...]
        else:
            c_ref = c1_ref if g == 1 else c2_ref
            kb, vb, bias_b = c_ref[:, 0], c_ref[:, 1], bb12_ref[g - 1]
            kn, vn, bias_n = qkv_ref[pl.ds(s, 1), g, 1], qkv_ref[pl.ds(s, 1), g, 2], bn12_ref[g - 1][None]
        lb = jnp.sum(q[None] * kb, axis=-1, keepdims=True) * ATT_SCALE + bias_b
        ln = jnp.sum(q[None] * kn, axis=-1, keepdims=True) * ATT_SCALE + bias_n
        mx = jnp.maximum(jnp.max(lb, axis=0), jnp.max(ln, axis=0))
        pb = jnp.exp(lb - mx[None])
        pn = jnp.exp(ln - mx[None])
        den = jnp.sum(pb, axis=0) + jnp.sum(pn, axis=0)
        acc = jnp.sum(pb * vb, axis=0) + jnp.sum(pn * vn, axis=0)
        outs.append(acc / den)
        lses.append(mx + jnp.log(den))
    mx = jnp.maximum(jnp.maximum(lses[0], lses[1]), lses[2])
    es = [jnp.exp(l - mx) for l in lses]
    tot = es[0] + es[1] + es[2]
    o_ref[...] = (es[0] / tot) * outs[0] + (es[1] / tot) * outs[1] + (es[2] / tot) * outs[2]


def _attn_a_sample(qkv_s, caches, bias_m, *, bs, s_new):
    h = N_A_HEADS
    lane = (h, HEAD_DIM)

    def dense(x):
        return jnp.broadcast_to(x[..., None], x.shape + (HEAD_DIM,)).astype(F32)

    sq = jnp.arange(s_new)
    r = jnp.arange(BAND)
    m_b = BAND + sq[:, None] - r[None, :]
    bb0 = dense(jnp.where((m_b <= BAND)[..., None], bias_m[0][jnp.clip(m_b, 0, BAND)], MASKED))
    m_n = sq[:, None] - sq[None, :]
    bn0 = dense(jnp.where((m_n >= 0)[..., None], bias_m[0][jnp.clip(m_n, 0, BAND)], MASKED))
    bb12 = dense(jnp.stack([bias_m[g][BAND - r] for g in (1, 2)]))
    bn12 = dense(jnp.stack([bias_m[g][0] for g in (1, 2)]))

    c0 = caches[0]
    c1 = caches[1].reshape(bs, BAND, DILATED_PAIRS[1][1], 2, h, HEAD_DIM)
    c2 = caches[2].reshape(bs, BAND, DILATED_PAIRS[2][1], 2, h, HEAD_DIM)
    return pl.pallas_call(
        _attn_a_sample_body,
        grid=(bs, s_new),
        in_specs=[
            pl.BlockSpec((None, s_new, N_GROUPS, 3) + lane, lambda b, s: (b, 0, 0, 0, 0, 0)),
            pl.BlockSpec((None, BAND, 2) + lane, lambda b, s: (b, 0, 0, 0, 0)),
            pl.BlockSpec((None, BAND, None, 2) + lane, lambda b, s: (b, 0, s, 0, 0, 0)),
            pl.BlockSpec((None, BAND, None, 2) + lane, lambda b, s: (b, 0, s, 0, 0, 0)),
            pl.BlockSpec((None, BAND) + lane, lambda b, s: (s, 0, 0, 0)),
            pl.BlockSpec((2, BAND) + lane, lambda b, s: (0, 0, 0, 0)),
            pl.BlockSpec((None, s_new) + lane, lambda b, s: (s, 0, 0, 0)),
            pl.BlockSpec((2,) + lane, lambda b, s: (0, 0, 0)),
        ],
        out_specs=pl.BlockSpec((None, None) + lane, lambda b, s: (b, s, 0, 0)),
        out_shape=jax.ShapeDtypeStruct((bs, s_new) + lane, F32),
        compiler_params=_params("arbitrary", "arbitrary"),
        name="attn_a_sample",
    )(qkv_s, c0, c1, c2, bb0, bb12, bn0, bn12)


def _softplus(z):
    return jnp.maximum(z, 0.0) + jnp.log(1.0 + jnp.exp(-jnp.abs(z)))


def _suffix_matrix():
    n = 2 * PAGE_SIZE
    j = lax.broadcasted_iota(jnp.int32, (n, n), 0) % PAGE_SIZE
    c = lax.broadcasted_iota(jnp.int32, (n, n), 1)
    return jnp.where((c >= PAGE_SIZE) | (j > c), 1.0, 0.0).astype(BF16)


def _hi_lo(l):
    hi = l.astype(BF16)
    lo = (l - hi.astype(F32)).astype(BF16)
    return jnp.concatenate([hi, lo], axis=1)


def _attn_b_prompt_body(q_ref, k_ref, v_ref, bias_ref, o_ref, kb_ref, vb_ref, *, t, tq):
    i = pl.program_id(2)

    @pl.when(i == 0)
    def _():
        def cast_step(c, carry):
            rows = pl.ds(pl.multiple_of(c * 256, 256), 256)
            kb_ref[rows, :] = (k_ref[rows, :] * ATT_SCALE).astype(BF16)
            vb_ref[rows, :] = v_ref[rows, :].astype(BF16)
            return carry
        lax.fori_loop(0, t // 256, cast_step, 0)

    suffix = _suffix_matrix()
    dn = (((1,), (1,)), ((), ()))
    heads = range(GQ)

    def sweep(jlast, nblk, state):
        krows = pl.ds(pl.multiple_of((jlast - (nblk - 1)) * PAGE_SIZE, PAGE_SIZE), nblk * PAGE_SIZE)
        k = kb_ref[krows, :]
        v = vb_ref[krows, :]
        diagonal = state is None
        if diagonal:
            qpos = lax.broadcasted_iota(jnp.int32, (tq, PAGE_SIZE), 0)
            kpos = lax.broadcasted_iota(jnp.int32, (tq, PAGE_SIZE), 1)
            visible = kpos < qpos
        zs = [lax.dot_general(q_ref[:, g * HEAD_DIM:(g + 1) * HEAD_DIM], k, dn, preferred_element_type=F32)
              + bias_ref[g, :, :nblk * PAGE_SIZE] for g in heads]
        sps = [_softplus(z) for z in zs]
        ls = [jnp.where(visible, -sp, 0.0) if diagonal else -sp for sp in sps]
        ss = [jnp.dot(jnp.concatenate([_hi_lo(l[:, u * PAGE_SIZE:(u + 1) * PAGE_SIZE])
                                       for u in reversed(range(nblk))], axis=0),
                      suffix, preferred_element_type=F32) for l in ls]
        laters, runs = [], []
        for g, s in zip(heads, ss):
            run = None if diagonal else state[0][g]
            later = [None] * nblk
            for idx in range(nblk):
                su = s[idx * tq:(idx + 1) * tq]
                later[nblk - 1 - idx] = su[:, :PAGE_SIZE] if run is None else su[:, :PAGE_SIZE] + run
                run = su[:, PAGE_SIZE:] if run is None else run + su[:, PAGE_SIZE:]
            laters.append(later[0] if nblk == 1 else jnp.concatenate(later, axis=1))
            runs.append(run)
        avals = [jnp.exp(z - sp + later) for z, sp, later in zip(zs, sps, laters)]
        if diagonal:
            avals = [jnp.where(visible, a, 0.0) for a in avals]
        pvs = [jnp.dot(a.astype(BF16), v, preferred_element_type=F32) for a in avals]
        accs = pvs if diagonal else [state[1][g] + pv for g, pv in zip(heads, pvs)]
        return tuple(runs), tuple(accs)

    odd = jnp.bitwise_and(i, 1)
    state = sweep(i, 1, None)
    state = lax.fori_loop(0, odd, lambda _, st: sweep(i - 1, 1, st), state)
    first_pair = i - 1 - odd
    _, accs = lax.fori_loop(0, lax.shift_right_logical(i, 1),
                            lambda p, st: sweep(first_pair - 2 * p, 2, st), state)
    for g in heads:
        o_ref[:, g * HEAD_DIM:(g + 1) * HEAD_DIM] = accs[g].astype(o_ref.dtype)


def _attn_b_prompt(q, kv, sb, *, b, t):
    tq = PAGE_SIZE
    nq = t // tq
    bias = jnp.broadcast_to(sb[:, :, None, None], (N_KV_HEADS, GQ, 1, 2 * PAGE_SIZE))
    return pl.pallas_call(
        functools.partial(_attn_b_prompt_body, t=t, tq=tq),
        grid=(b, N_KV_HEADS, nq),
        in_specs=[
            pl.BlockSpec((tq, GQ * HEAD_DIM), lambda bi, kh, i: (bi * nq + i, kh)),
            pl.BlockSpec((t, HEAD_DIM), lambda bi, kh, i: (bi, kh)),
            pl.BlockSpec((t, HEAD_DIM), lambda bi, kh, i: (bi, N_KV_HEADS + kh)),
            pl.BlockSpec((None, GQ, 1, 2 * PAGE_SIZE), lambda bi, kh, i: (kh, 0, 0, 0)),
        ],
        out_specs=pl.BlockSpec((tq, GQ * HEAD_DIM), lambda bi, kh, i: (bi * nq + i, kh)),
        out_shape=jax.ShapeDtypeStruct((b * t, N_B_HEADS * HEAD_DIM), BF16),
        scratch_shapes=[pltpu.VMEM((t, HEAD_DIM), BF16), pltpu.VMEM((t, HEAD_DIM), BF16)],
        compiler_params=_params("arbitrary", "arbitrary", "arbitrary"),
        name="attn_b_prompt",
    )(q, kv, kv, bias)


def _attn_b_sample_body(pt_ref, q_ref, kvn_ref, bias_ref, *rest, s_new, n_steps):
    page_refs = rest[:PAGES_PER_STEP]
    o_ref, run_ref, acc_ref = rest[PAGES_PER_STEP:]
    p = pl.program_id(1)
    rows = GQ * s_new
    row_tok = lax.broadcasted_iota(jnp.int32, (rows, HEAD_DIM), 0) % s_new

    @pl.when(p == 0)
    def _():
        for kh in range(N_KV_HEADS):
            q = q_ref[kh]
            run = jnp.zeros((rows, HEAD_DIM), F32)
            acc = jnp.zeros((rows, HEAD_DIM), F32)
            for j in reversed(range(s_new)):
                kj = kvn_ref[0, kh, pl.ds(j, 1), :]
                vj = kvn_ref[1, kh, pl.ds(j, 1), :]
                z = jnp.sum(q * kj, axis=-1, keepdims=True) * ATT_SCALE + bias_ref[kh]
                valid = row_tok > j
                sp = _softplus(z)
                a = jnp.where(valid, jnp.exp(z - sp + run), 0.0)
                acc = acc + a * vj
                run = run + jnp.where(valid, -sp, 0.0)
            run_ref[kh] = run
            acc_ref[kh] = acc

    suffix = _suffix_matrix()
    stride = 2 * N_KV_HEADS
    dn = (((1,), (1,)), ((), ()))
    def head_rows(first):
        return jnp.concatenate([r[pl.ds(first, PAGE_SIZE, stride=stride), :] for r in page_refs], axis=0).astype(BF16)

    def later_and_run(s, run):
        later = []
        for u in range(PAGES_PER_STEP):
            su = s[u * rows:(u + 1) * rows]
            later.append(su[:, :PAGE_SIZE] + run)
            run = run + su[:, PAGE_SIZE:]
        return jnp.concatenate(later, axis=1), run

    heads = range(N_KV_HEADS)
    zs = [lax.dot_general(q_ref[kh].astype(BF16), head_rows(kh), dn, preferred_element_type=F32) * ATT_SCALE
          + jnp.tile(bias_ref[kh], (1, PAGES_PER_STEP)) for kh in heads]
    sps = [_softplus(z) for z in zs]
    ss = [jnp.dot(jnp.concatenate([_hi_lo(-sp[:, u * PAGE_SIZE:(u + 1) * PAGE_SIZE])
                                   for u in range(PAGES_PER_STEP)], axis=0),
                  suffix, preferred_element_type=F32) for sp in sps]
    lrs = [later_and_run(s, run_ref[kh]) for kh, s in zip(heads, ss)]
    avals = [jnp.exp(z - sp + later) for z, sp, (later, _) in zip(zs, sps, lrs)]
    pvs = [jnp.dot(a.astype(BF16), head_rows(N_KV_HEADS + kh), preferred_element_type=F32)
           for kh, a in zip(heads, avals)]
    for kh in heads:
        acc_ref[kh] += pvs[kh]
        run_ref[kh] = lrs[kh][1]

    @pl.when(p == n_steps - 1)
    def _():
        o_ref[...] = acc_ref[...]


def _attn_b_sample(q_rows, kv_new, sb, cache_kv, page_table, *, s_new):
    bs, n_pages = page_table.shape
    assert n_pages % PAGES_PER_STEP == 0
    n_steps = n_pages // PAGES_PER_STEP
    rows = GQ * s_new
    bias = jnp.broadcast_to(sb[:, :, None, None], (N_KV_HEADS, GQ, s_new, HEAD_DIM)).reshape(
        N_KV_HEADS, rows, HEAD_DIM)
    pool = cache_kv.reshape(cache_kv.shape[0], PAGE_SIZE * 2 * N_KV_HEADS, HEAD_DIM)

    def page_spec(u):
        return pl.BlockSpec((None, PAGE_SIZE * 2 * N_KV_HEADS, HEAD_DIM),
                            lambda b, p, pt: (pt[b, n_pages - 1 - (p * PAGES_PER_STEP + u)], 0, 0))

    grid_spec = pltpu.PrefetchScalarGridSpec(
        num_scalar_prefetch=1,
        grid=(bs, n_steps),
        in_specs=[
            pl.BlockSpec((None, N_KV_HEADS, rows, HEAD_DIM), lambda b, p, pt: (b, 0, 0, 0)),
            pl.BlockSpec((None, 2, N_KV_HEADS, s_new, HEAD_DIM), lambda b, p, pt: (b, 0, 0, 0, 0)),
            pl.BlockSpec((N_KV_HEADS, rows, HEAD_DIM), lambda b, p, pt: (0, 0, 0)),
        ] + [page_spec(u) for u in range(PAGES_PER_STEP)],
        out_specs=pl.BlockSpec((None, N_KV_HEADS, rows, HEAD_DIM), lambda b, p, pt: (b, 0, 0, 0)),
        scratch_shapes=[pltpu.VMEM((N_KV_HEADS, rows, HEAD_DIM), F32),
                        pltpu.VMEM((N_KV_HEADS, rows, HEAD_DIM), F32)],
    )
    return pl.pallas_call(
        functools.partial(_attn_b_sample_body, s_new=s_new, n_steps=n_steps),
        grid_spec=grid_spec,
        out_shape=jax.ShapeDtypeStruct((bs, N_KV_HEADS, rows, HEAD_DIM), F32),
        compiler_params=_params("arbitrary", "arbitrary"),
        name="attn_b_sample",
    )(page_table, q_rows, kv_new, bias, *([pool] * PAGES_PER_STEP))


def _mlp(x, gain, w_up, w_down, *, tiles, layer=0, emit_w=False):
    h = _norm_matmul(x, gain, w_up, mode="relu2", out_dtype=BF16, tiles=tiles, layer=layer, emit_w=emit_w)
    if emit_w:
        h, w_up_b = h
        y, w_down_b = _matmul_res(h, w_down, x, tiles=tiles, layer=layer, emit_w=True)
        return y, w_up_b, w_down_b
    return _matmul_res(h, w_down, x, tiles=tiles)


@jax.jit
def _step(x_prompt, x_sample, cache_win_g0, cache_win_g1, cache_win_g2, cache_kv, page_table, bias_table,
          norm_mix, norm_mlp, w_a_in, g_a_q, g_a_k, w_a_out, norm_kv, w_kv, w_b_q, b_sb, w_b_out, w_up, w_down):
    bp, tp, _ = x_prompt.shape
    bs, ts, _ = x_sample.shape
    pt, st = PROMPT_TILES, _sample_tiles(bs * ts)
    xp = x_prompt.reshape(bp * tp, D_MODEL)
    xs = x_sample.reshape(bs * ts, D_MODEL)
    caches = (cache_win_g0[0], cache_win_g1[0], cache_win_g2[0])

    head_gain = jnp.concatenate([jnp.tile(g_a_q[0], N_A_HEADS), jnp.tile(g_a_k[0], N_A_HEADS),
                                 jnp.ones((A_WIDTH,), F32)])
    col_gain = jnp.tile(head_gain, N_GROUPS).reshape(1, -1).astype(F32)
    qkv_s, w_in = _norm_matmul(xs, norm_mix[0], w_a_in, mode="qkv", out_dtype=F32, tiles=st,
                               col_gain=col_gain, emit_w=True)
    qkv_p = _norm_matmul(xp, norm_mix[0], w_in, mode="qkv", out_dtype=F32, tiles=pt, col_gain=col_gain)

    bias_m = _bias_by_stride(bias_table)
    qkv_s6 = qkv_s.reshape(bs, ts, N_GROUPS, 3, N_A_HEADS, HEAD_DIM)
    o_s = _attn_a_sample(qkv_s6, caches, bias_m, bs=bs, s_new=ts)
    o_s = o_s.reshape(bs * ts, A_WIDTH).astype(BF16)
    o_p = _attn_a_prompt(qkv_p, _prompt_bias(bias_m), b=bp, t=tp)

    xs, w_ao = _matmul_res(o_s, w_a_out, xs, tiles=st, emit_w=True)
    xp = _matmul_res(o_p, w_ao, xp, tiles=pt)
    xs, w_up0, w_dn0 = _mlp(xs, norm_mlp[0], w_up, w_down, tiles=st, layer=0, emit_w=True)
    xp = _mlp(xp, norm_mlp[0], w_up0, w_dn0, tiles=pt)

    kv_s, w_kvb = _norm_matmul(xs, norm_kv, w_kv, mode="plain", out_dtype=F32, tiles=st, emit_w=True)
    kv_p = _norm_matmul(xp, norm_kv, w_kvb, mode="plain", out_dtype=F32, tiles=pt)

    qb_s, w_bq = _norm_matmul(xs, norm_mix[1], w_b_q, mode="plain", out_dtype=F32, tiles=st, emit_w=True)
    qb_p = _norm_matmul(xp, norm_mix[1], w_bq, mode="plain", out_dtype=BF16, tiles=pt)

    sb = b_sb[0].astype(F32).reshape(N_KV_HEADS, GQ)
    q_rows = qb_s.reshape(bs, ts, N_KV_HEADS, GQ, HEAD_DIM).transpose(0, 2, 3, 1, 4).reshape(
        bs, N_KV_HEADS, GQ * ts, HEAD_DIM)
    kv_s5 = kv_s.reshape(bs, ts, 2, N_KV_HEADS, HEAD_DIM)
    ob_s = _attn_b_sample(q_rows, kv_s5.transpose(0, 2, 3, 1, 4), sb, cache_kv, page_table, s_new=ts)
    ob_s = ob_s.reshape(bs, N_KV_HEADS, GQ, ts, HEAD_DIM).transpose(0, 3, 1, 2, 4).reshape(
        bs * ts, N_B_HEADS * HEAD_DIM).astype(BF16)
    ob_p = _attn_b_prompt(qb_p, kv_p, sb, b=bp, t=tp)

    xs, w_bo = _matmul_res(ob_s, w_b_out, xs, tiles=st, emit_w=True)
    xp = _matmul_res(ob_p, w_bo, xp, tiles=pt)
    xs, w_up1, w_dn1 = _mlp(xs, norm_mlp[1], w_up, w_down, tiles=st, layer=1, emit_w=True)
    xp = _mlp(xp, norm_mlp[1], w_up1, w_dn1, tiles=pt)

    qkv_p6 = qkv_p.reshape(bp, tp, N_GROUPS, 3, N_A_HEADS, HEAD_DIM)
    win_p, win_s = [], []
    for g, (window, _) in enumerate(DILATED_PAIRS):
        wp = min(window, tp)
        win_p.append(qkv_p6[:, tp - wp:, g, 1:3][None])
        win_s.append(jnp.concatenate([caches[g][:, ts:], qkv_s6[:, :, g, 1:3]], axis=1)[None])
    return (xp.reshape(bp, tp, D_MODEL), xs.reshape(bs, ts, D_MODEL), *win_p, *win_s,
            kv_p.reshape(bp, tp, 2, N_KV_HEADS, HEAD_DIM), kv_s5)


def kernel(x_prompt, x_sample, cache_win_g0, cache_win_g1, cache_win_g2, cache_kv, page_table, bias_table,
           norm_mix, norm_mlp, w_a_in, g_a_q, g_a_k, w_a_out, norm_kv, w_kv, w_b_q, b_sb, w_b_out, w_up, w_down):
    return _step(x_prompt, x_sample, cache_win_g0, cache_win_g1, cache_win_g2, cache_kv, page_table, bias_table,
                 norm_mix, norm_mlp, w_a_in, g_a_q, g_a_k, w_a_out, norm_kv, w_kv, w_b_q, b_sb, w_b_out,
                 w_up, w_down)
```

```python
import functools
import math
from typing import NamedTuple

import jax
import jax.numpy as jnp
from jax import lax
from jax.experimental import pallas as pl
from jax.experimental.pallas import tpu as pltpu

F32 = jnp.float32
BF16 = jnp.bfloat16

D_MODEL = 4096
HEAD_DIM = 128
DILATED_PAIRS = ((128, 1), (512, 4), (2048, 16))
N_GROUPS = len(DILATED_PAIRS)
N_A_HEADS = D_MODEL // (2 * HEAD_DIM)
A_WIDTH = N_A_HEADS * HEAD_DIM
N_B_HEADS = D_MODEL // HEAD_DIM
N_KV_HEADS = 8
GQ = N_B_HEADS // N_KV_HEADS
N_BUCKETS = 32
MAX_DISTANCE = 2048
RMS_EPS = 1e-6
PAGE_SIZE = 128
BAND = 128
ATT_SCALE = HEAD_DIM ** -0.5
MASKED = -1e30

V7X_VMEM_LIMIT_BYTES = 60 * 1024 * 1024
V7X_MXU_COLS = 256
NORM_ROWS = 16
PAGES_PER_STEP = 4


class Tiles(NamedTuple):
    tm: int
    tn_full_k: int
    tn_res: int
    tk_res: int
    tr_norm: int


PROMPT_TILES = Tiles(tm=1024, tn_full_k=1024, tn_res=1024, tk_res=4096, tr_norm=512)


def _sample_tiles(rows):
    return Tiles(tm=rows, tn_full_k=512, tn_res=1024, tk_res=2048, tr_norm=rows)


def _params(*semantics):
    return pltpu.CompilerParams(dimension_semantics=semantics,
                                vmem_limit_bytes=V7X_VMEM_LIMIT_BYTES)


def _weight_spec(w, layer, block, index):
    if w.ndim == 2:
        return pl.BlockSpec(block, index)
    return pl.BlockSpec((None,) + block, lambda *g: (layer,) + index(*g))


def _head_rms(t, cg):
    ms = jnp.mean(t * t, axis=-1, keepdims=True)
    return t * lax.rsqrt(ms + RMS_EPS) * cg


def _rms_rows(x_ref, g_refs, o_refs, n_rows):
    per = math.gcd(n_rows // NORM_ROWS, 4)

    def rows_step(c, carry):
        groups = [pl.ds(pl.multiple_of((c * per + u) * NORM_ROWS, NORM_ROWS), NORM_ROWS) for u in range(per)]
        scales = [lax.rsqrt(jnp.mean(jnp.square(x_ref[rows, :]), axis=-1, keepdims=True) + RMS_EPS)
                  for rows in groups]
        for rows, scale in zip(groups, scales):
            y = x_ref[rows, :] * scale
            for g_ref, o_ref in zip(g_refs, o_refs):
                o_ref[rows, :] = (y * g_ref[...]).astype(BF16)
        return carry

    lax.fori_loop(0, n_rows // (NORM_ROWS * per), rows_step, 0)


def _rms_norm_body(x_ref, *refs, n_gains, tr):
    _rms_rows(x_ref, refs[:n_gains], refs[n_gains:], tr)


def _rms_norm(x, gains, *, tr):
    m, k = x.shape
    n = len(gains)
    outs = pl.pallas_call(
        functools.partial(_rms_norm_body, n_gains=n, tr=tr),
        grid=(m // tr,),
        in_specs=[pl.BlockSpec((tr, k), lambda i: (i, 0))] + [pl.BlockSpec((1, k), lambda i: (0, 0))] * n,
        out_specs=[pl.BlockSpec((tr, k), lambda i: (i, 0))] * n,
        out_shape=[jax.ShapeDtypeStruct((m, k), BF16)] * n,
        compiler_params=_params("arbitrary"),
        name=f"rms_norm_{n}",
    )(x, *[g.reshape(1, k) for g in gains])
    return outs


def _norm_matmul_body(x_ref, *refs, mode, tm, tn, emit_w, fused_norm):
    if fused_norm:
        g_ref, w_ref, cg_ref, o_ref, *rest = refs
    else:
        w_ref, cg_ref, o_ref, *rest = refs
    if emit_w:
        wb_ref, *rest = rest
    j = pl.program_id(1)

    if fused_norm:
        (xn_ref,) = rest

        @pl.when(j == 0)
        def _():
            _rms_rows(x_ref, [g_ref], [xn_ref], tm)
    else:
        xn_ref = x_ref

    if emit_w:
        wb_ref[...] = w_ref[...].astype(BF16)
        w_src = wb_ref
    else:
        w_src = w_ref

    if mode == "relu2":
        r = jnp.maximum(jnp.dot(xn_ref[...], w_src[...], preferred_element_type=F32), 0.0)
        o_ref[...] = (r * r).astype(o_ref.dtype)
    elif mode == "plain":
        o_ref[...] = jnp.dot(xn_ref[...], w_src[...], preferred_element_type=F32).astype(o_ref.dtype)
    else:
        kind = (j * tn // A_WIDTH) % 3

        @pl.when(kind == 2)
        def _():
            o_ref[...] = jnp.dot(xn_ref[...], w_src[...], preferred_element_type=F32)

        @pl.when(kind != 2)
        def _():
            for c0 in range(0, tn, V7X_MXU_COLS):
                y = jnp.dot(xn_ref[...], w_src[:, c0:c0 + V7X_MXU_COLS], preferred_element_type=F32)
                for c in range(0, V7X_MXU_COLS, HEAD_DIM):
                    cols = slice(c0 + c, c0 + c + HEAD_DIM)
                    o_ref[:, cols] = _head_rms(y[:, c:c + HEAD_DIM], cg_ref[:, cols])


def _norm_matmul(x, gain, w, *, mode, out_dtype, tiles, layer=0, col_gain=None, emit_w=False):
    m, k = x.shape
    n = w.shape[-1]
    tm, tn = tiles.tm, tiles.tn_full_k
    fused_norm = gain is not None
    assert not emit_w or m == tm
    assert fused_norm or x.dtype == BF16
    if col_gain is None:
        col_gain = jnp.ones((1, n), F32)
    body = functools.partial(_norm_matmul_body, mode=mode, tm=tm, tn=tn, emit_w=emit_w, fused_norm=fused_norm)
    out_specs = pl.BlockSpec((tm, tn), lambda i, j: (i, j))
    out_shape = jax.ShapeDtypeStruct((m, n), out_dtype)
    if emit_w:
        out_specs = [out_specs, pl.BlockSpec((k, tn), lambda i, j: (0, j))]
        out_shape = [out_shape, jax.ShapeDtypeStruct((k, n), BF16)]
    if fused_norm:
        x_specs = [pl.BlockSpec((tm, k), lambda i, j: (i, 0), pipeline_mode=pl.Buffered(1)),
                   pl.BlockSpec((1, k), lambda i, j: (0, 0))]
        x_args = (x, gain.reshape(1, k))
        scratch = [pltpu.VMEM((tm, k), BF16)]
    else:
        x_specs = [pl.BlockSpec((tm, k), lambda i, j: (i, 0))]
        x_args = (x,)
        scratch = []
    return pl.pallas_call(
        body,
        grid=(m // tm, n // tn),
        in_specs=x_specs + [
            _weight_spec(w, layer, (k, tn), lambda i, j: (0, j)),
            pl.BlockSpec((1, tn), lambda i, j: (0, j)),
        ],
        out_specs=out_specs,
        out_shape=out_shape,
        scratch_shapes=scratch,
        compiler_params=_params("arbitrary", "arbitrary"),
        name=f"norm_matmul_{mode}_{tm}",
    )(*x_args, w, col_gain)


def _matmul_res_body(a_ref, w_ref, r_ref, o_ref, *rest, emit_w):
    kk = pl.program_id(2)
    if emit_w:
        (wb_ref,) = rest
        wb_ref[...] = w_ref[...].astype(BF16)
        w_src = wb_ref
    else:
        w_src = w_ref
    d = jnp.dot(a_ref[...], w_src[...], preferred_element_type=F32)

    @pl.when(kk == 0)
    def _():
        o_ref[...] = r_ref[...] + d

    @pl.when(kk != 0)
    def _():
        o_ref[...] += d


def _matmul_res(a, w, res, *, tiles, layer=0, emit_w=False):
    m, k = a.shape
    n = w.shape[-1]
    tm, tn, tk = tiles.tm, tiles.tn_res, min(tiles.tk_res, k)
    assert not emit_w or m == tm
    out_specs = pl.BlockSpec((tm, tn), lambda i, j, kk: (i, j))
    out_shape = jax.ShapeDtypeStruct((m, n), F32)
    if emit_w:
        out_specs = [out_specs, pl.BlockSpec((tk, tn), lambda i, j, kk: (kk, j))]
        out_shape = [out_shape, jax.ShapeDtypeStruct((k, n), BF16)]
    return pl.pallas_call(
        functools.partial(_matmul_res_body, emit_w=emit_w),
        grid=(m // tm, n // tn, k // tk),
        in_specs=[
            pl.BlockSpec((tm, tk), lambda i, j, kk: (i, kk)),
            _weight_spec(w, layer, (tk, tn), lambda i, j, kk: (kk, j)),
            pl.BlockSpec((tm, tn), lambda i, j, kk: (i, j)),
        ],
        out_specs=out_specs,
        out_shape=out_shape,
        compiler_params=_params("arbitrary", "arbitrary", "arbitrary"),
        name=f"matmul_res_{tm}_{k}",
    )(a, w, res)


def _rel_bucket(dist):
    max_exact = N_BUCKETS // 2
    d32 = jnp.maximum(dist, 1).astype(F32)
    large = max_exact + (jnp.log(d32 / max_exact) / math.log(MAX_DISTANCE / max_exact)
                         * (N_BUCKETS - max_exact)).astype(jnp.int32)
    large = jnp.minimum(large, N_BUCKETS - 1)
    return jnp.where(dist < max_exact, dist, large)


def _bias_by_stride(bias_table):
    m = jnp.arange(BAND + 1)
    return jnp.stack([bias_table[:, g][_rel_bucket(m * d)] for g, (_, d) in enumerate(DILATED_PAIRS)])


def _prompt_bias(bias_m):
    period = 2 * BAND + 1
    u = jnp.concatenate([bias_m[:, ::-1, :], jnp.full((N_GROUPS, period - BAND - 1, N_A_HEADS), MASKED, F32)],
                        axis=1).transpose(0, 2, 1)
    flat = jnp.tile(u, (1, 1, BAND))[:, :, :BAND * 2 * BAND]
    return flat.reshape(N_GROUPS, N_A_HEADS, BAND, 2 * BAND)


def _attn_a_prompt_body(q0, k0, v0, q1, k1, v1, q2, k2, v2, bias_ref, o_ref, og_ref, lse_ref, *, t):
    qkv = ((q0, k0, v0), (q1, k1, v1), (q2, k2, v2))
    ones = jnp.ones((BAND, HEAD_DIM), BF16)
    dn = (((1,), (1,)), ((), ()))

    def rows_of(start, d):
        if d == 1:
            return pl.ds(start, BAND)
        return pl.ds(start, BAND, stride=d)

    def blocks(g, specs):
        q_ref, k_ref, v_ref = qkv[g]
        with_prev = specs[0][1] is not None

        def logits(q, rows, bias):
            return lax.dot_general(q, k_ref[rows, :].astype(BF16), dn, preferred_element_type=F32) * ATT_SCALE + bias

        def weighted(p, rows):
            vx = jnp.concatenate([v_ref[rows, :].astype(BF16), ones], axis=1)
            return jnp.dot(p.astype(BF16), vx, preferred_element_type=F32)

        qs = [q_ref[cur, :].astype(BF16) for cur, _ in specs]
        s_cs = [logits(q, cur, bias_ref[g, :, BAND:]) for q, (cur, _) in zip(qs, specs)]
        mxs = [jnp.max(s, axis=-1, keepdims=True) for s in s_cs]
        if with_prev:
            s_ps = [logits(q, prev, bias_ref[g, :, :BAND]) for q, (_, prev) in zip(qs, specs)]
            mxs = [jnp.maximum(mx, jnp.max(s, axis=-1, keepdims=True)) for mx, s in zip(mxs, s_ps)]
        accs = [weighted(jnp.exp(s - mx), cur) for s, mx, (cur, _) in zip(s_cs, mxs, specs)]
        if with_prev:
            accs = [acc + weighted(jnp.exp(s - mx), prev) for acc, s, mx, (_, prev) in zip(accs, s_ps, mxs, specs)]
        for acc, mx, (cur, _) in zip(accs, mxs, specs):
            den = acc[:, HEAD_DIM:]
            og_ref[g, cur, :] = acc[:, :HEAD_DIM] / den
            lse_ref[g, cur, :] = mx + jnp.log(den)

    for g, (_, d) in enumerate(DILATED_PAIRS):
        span = BAND * d
        n_blk = t // span
        if n_blk == 1:
            per = 4

            def residue_step(it, carry, g=g, d=d, per=per):
                blocks(g, [(rows_of(it * per + u, d), None) for u in range(per)])
                return carry
            lax.fori_loop(0, d // per, residue_step, 0)
        elif d == 1:
            per = 5
            assert (n_blk - 1) % per == 0
            blocks(g, [(rows_of(0, d), None)])

            def blk_step(it, carry, g=g, d=d, per=per, span=span):
                starts = [pl.multiple_of((1 + it * per + u) * span, BAND) for u in range(per)]
                blocks(g, [(rows_of(st, d), rows_of(st - span, d)) for st in starts])
                return carry
            lax.fori_loop(0, (n_blk - 1) // per, blk_step, 0)
        else:
            blocks(g, [(rows_of(r, d), None) for r in range(d)])

            def blk_step(n, carry, g=g, d=d, span=span):
                blocks(g, [(rows_of(n * span + r, d), rows_of(n * span + r - span, d)) for r in range(d)])
                return carry
            lax.fori_loop(1, n_blk, blk_step, 0)

    def merge_step(c, carry):
        rows = pl.ds(pl.multiple_of(c * BAND, BAND), BAND)
        l0, l1, l2 = lse_ref[0, rows, :], lse_ref[1, rows, :], lse_ref[2, rows, :]
        mx = jnp.maximum(jnp.maximum(l0, l1), l2)
        e0, e1, e2 = jnp.exp(l0 - mx), jnp.exp(l1 - mx), jnp.exp(l2 - mx)
        tot = e0 + e1 + e2
        o = (e0 / tot) * og_ref[0, rows, :] + (e1 / tot) * og_ref[1, rows, :] + (e2 / tot) * og_ref[2, rows, :]
        o_ref[rows, :] = o.astype(o_ref.dtype)
        return carry

    lax.fori_loop(0, t // BAND, merge_step, 0)


def _attn_a_prompt(qkv, bias, *, b, t):
    heads = N_A_HEADS

    def col_spec(g, c):
        base = (g * 3 + c) * heads
        return pl.BlockSpec((t, HEAD_DIM), lambda bi, h: (bi, base + h))

    in_specs = [col_spec(g, c) for g in range(N_GROUPS) for c in range(3)]
    in_specs.append(pl.BlockSpec((N_GROUPS, None, BAND, 2 * BAND), lambda bi, h: (0, h, 0, 0)))
    return pl.pallas_call(
        functools.partial(_attn_a_prompt_body, t=t),
        grid=(b, heads),
        in_specs=in_specs,
        out_specs=pl.BlockSpec((t, HEAD_DIM), lambda bi, h: (bi, h)),
        out_shape=jax.ShapeDtypeStruct((b * t, A_WIDTH), BF16),
        scratch_shapes=[pltpu.VMEM((N_GROUPS, t, HEAD_DIM), F32),
                        pltpu.VMEM((N_GROUPS, t, HEAD_DIM), F32)],
        compiler_params=_params("arbitrary", "arbitrary"),
        name="attn_a_prompt",
    )(*([qkv] * 9), bias)


def _attn_a_sample_body(qkv_ref, c0_ref, c1_ref, c2_ref, bb0_ref, bb12_ref, bn0_ref, bn12_ref, o_ref):
    s = pl.program_id(1)
    outs, lses = [], []
    for g in range(N_GROUPS):
        q = qkv_ref[s, g, 0]
        if g == 0:
            kb, vb, bias_b = c0_ref[:, 0], c0_ref[:, 1], bb0_ref[...]
            kn, vn, bias_n = qkv_ref[:, 0, 1], qkv_ref[:, 0, 2], bn0_ref[...]
        else:
            c_ref = c1_ref if g == 1 else c2_ref
            kb, vb, bias_b = c_ref[:, 0], c_ref[:, 1], bb12_ref[g - 1]
            kn, vn, bias_n = qkv_ref[pl.ds(s, 1), g, 1], qkv_ref[pl.ds(s, 1), g, 2], bn12_ref[g - 1][None]
        lb = jnp.sum(q[None] * kb, axis=-1, keepdims=True) * ATT_SCALE + bias_b
        ln = jnp.sum(q[None] * kn, axis=-1, keepdims=True) * ATT_SCALE + bias_n
        mx = jnp.maximum(jnp.max(lb, axis=0), jnp.max(ln, axis=0))
        pb = jnp.exp(lb - mx[None])
        pn = jnp.exp(ln - mx[None])
        den = jnp.sum(pb, axis=0) + jnp.sum(pn, axis=0)
        acc = jnp.sum(pb * vb, axis=0) + jnp.sum(pn * vn, axis=0)
        outs.append(acc / den)
        lses.append(mx + jnp.log(den))
    mx = jnp.maximum(jnp.maximum(lses[0], lses[1]), lses[2])
    es = [jnp.exp(l - mx) for l in lses]
    tot = es[0] + es[1] + es[2]
    o_ref[...] = (es[0] / tot) * outs[0] + (es[1] / tot) * outs[1] + (es[2] / tot) * outs[2]


def _attn_a_sample(qkv_s, caches, bias_m, *, bs, s_new):
    h = N_A_HEADS
    lane = (h, HEAD_DIM)

    def dense(x):
        return jnp.broadcast_to(x[..., None], x.shape + (HEAD_DIM,)).astype(F32)

    sq = jnp.arange(s_new)
    r = jnp.arange(BAND)
    m_b = BAND + sq[:, None] - r[None, :]
    bb0 = dense(jnp.where((m_b <= BAND)[..., None], bias_m[0][jnp.clip(m_b, 0, BAND)], MASKED))
    m_n = sq[:, None] - sq[None, :]
    bn0 = dense(jnp.where((m_n >= 0)[..., None], bias_m[0][jnp.clip(m_n, 0, BAND)], MASKED))
    bb12 = dense(jnp.stack([bias_m[g][BAND - r] for g in (1, 2)]))
    bn12 = dense(jnp.stack([bias_m[g][0] for g in (1, 2)]))

    c0 = caches[0]
    c1 = caches[1].reshape(bs, BAND, DILATED_PAIRS[1][1], 2, h, HEAD_DIM)
    c2 = caches[2].reshape(bs, BAND, DILATED_PAIRS[2][1], 2, h, HEAD_DIM)
    return pl.pallas_call(
        _attn_a_sample_body,
        grid=(bs, s_new),
        in_specs=[
            pl.BlockSpec((None, s_new, N_GROUPS, 3) + lane, lambda b, s: (b, 0, 0, 0, 0, 0)),
            pl.BlockSpec((None, BAND, 2) + lane, lambda b, s: (b, 0, 0, 0, 0)),
            pl.BlockSpec((None, BAND, None, 2) + lane, lambda b, s: (b, 0, s, 0, 0, 0)),
            pl.BlockSpec((None, BAND, None, 2) + lane, lambda b, s: (b, 0, s, 0, 0, 0)),
            pl.BlockSpec((None, BAND) + lane, lambda b, s: (s, 0, 0, 0)),
            pl.BlockSpec((2, BAND) + lane, lambda b, s: (0, 0, 0, 0)),
            pl.BlockSpec((None, s_new) + lane, lambda b, s: (s, 0, 0, 0)),
            pl.BlockSpec((2,) + lane, lambda b, s: (0, 0, 0)),
        ],
        out_specs=pl.BlockSpec((None, None) + lane, lambda b, s: (b, s, 0, 0)),
        out_shape=jax.ShapeDtypeStruct((bs, s_new) + lane, F32),
        compiler_params=_params("arbitrary", "arbitrary"),
        name="attn_a_sample",
    )(qkv_s, c0, c1, c2, bb0, bb12, bn0, bn12)


LOG2E = 1.0 / math.log(2.0)
SB_LOGIT_SCALE = ATT_SCALE * LOG2E


def _softplus2(z2):
    return jnp.maximum(z2, 0.0) + jnp.log2(1.0 + jnp.exp2(-jnp.abs(z2)))


def _suffix_matrix():
    n = 2 * PAGE_SIZE
    j = lax.broadcasted_iota(jnp.int32, (n, n), 0) % PAGE_SIZE
    c = lax.broadcasted_iota(jnp.int32, (n, n), 1)
    return jnp.where((c >= PAGE_SIZE) | (j > c), 1.0, 0.0).astype(BF16)


def _hi_lo(l):
    hi = l.astype(BF16)
    lo = (l - hi.astype(F32)).astype(BF16)
    return jnp.concatenate([hi, lo], axis=1)


def _attn_b_prompt_body(q_ref, k_ref, v_ref, bias_ref, o_ref, kb_ref, vb_ref, *, t, tq):
    i = pl.program_id(2)

    @pl.when(i == 0)
    def _():
        def cast_step(c, carry):
            rows = pl.ds(pl.multiple_of(c * 256, 256), 256)
            kb_ref[rows, :] = (k_ref[rows, :] * SB_LOGIT_SCALE).astype(BF16)
            vb_ref[rows, :] = v_ref[rows, :].astype(BF16)
            return carry
        lax.fori_loop(0, t // 256, cast_step, 0)

    suffix = _suffix_matrix()
    dn = (((1,), (1,)), ((), ()))
    heads = range(GQ)

    def sweep(jlast, nblk, state):
        krows = pl.ds(pl.multiple_of((jlast - (nblk - 1)) * PAGE_SIZE, PAGE_SIZE), nblk * PAGE_SIZE)
        k = kb_ref[krows, :]
        v = vb_ref[krows, :]
        diagonal = state is None
        if diagonal:
            qpos = lax.broadcasted_iota(jnp.int32, (tq, PAGE_SIZE), 0)
            kpos = lax.broadcasted_iota(jnp.int32, (tq, PAGE_SIZE), 1)
            visible = kpos < qpos
        zs = [lax.dot_general(q_ref[:, g * HEAD_DIM:(g + 1) * HEAD_DIM], k, dn, preferred_element_type=F32)
              + bias_ref[g, :, :nblk * PAGE_SIZE] for g in heads]
        sps = [_softplus2(z) for z in zs]
        ls =[jnp.where(visible, -sp, 0.0) if diagonal else -sp for sp in sps]
        ss = [jnp.dot(jnp.concatenate([_hi_lo(l[:, u * PAGE_SIZE:(u + 1) * PAGE_SIZE])
                                       for u in reversed(range(nblk))], axis=0),
                      suffix, preferred_element_type=F32) for l in ls]
        laters, runs = [], []
        for g, s in zip(heads, ss):
            run = None if diagonal else state[0][g]
            later = [None] * nblk
            for idx in range(nblk):
                su = s[idx * tq:(idx + 1) * tq]
                later[nblk - 1 - idx] = su[:, :PAGE_SIZE] if run is None else su[:, :PAGE_SIZE] + run
                run = su[:, PAGE_SIZE:] if run is None else run + su[:, PAGE_SIZE:]
            laters.append(later[0] if nblk == 1 else jnp.concatenate(later, axis=1))
            runs.append(run)
        avals = [jnp.exp2(z - sp + later) for z, sp, later in zip(zs, sps, laters)]
        if diagonal:
            avals = [jnp.where(visible, a, 0.0) for a in avals]
        pvs = [jnp.dot(a.astype(BF16), v, preferred_element_type=F32) for a in avals]
        accs = pvs if diagonal else [state[1][g] + pv for g, pv in zip(heads, pvs)]
        return tuple(runs), tuple(accs)

    odd = jnp.bitwise_and(i, 1)
    state = sweep(i, 1, None)
    state = lax.fori_loop(0, odd, lambda _, st: sweep(i - 1, 1, st), state)
    first_pair = i - 1 - odd
    _, accs = lax.fori_loop(0, lax.shift_right_logical(i, 1),
                            lambda p, st: sweep(first_pair - 2 * p, 2, st), state)
    for g in heads:
        o_ref[:, g * HEAD_DIM:(g + 1) * HEAD_DIM] = accs[g].astype(o_ref.dtype)


def _attn_b_prompt(q, kv, sb, *, b, t):
    tq = PAGE_SIZE
    nq = t // tq
    bias = jnp.broadcast_to(sb[:, :, None, None], (N_KV_HEADS, GQ, 1, 2 * PAGE_SIZE))
    return pl.pallas_call(
        functools.partial(_attn_b_prompt_body, t=t, tq=tq),
        grid=(b, N_KV_HEADS, nq),
        in_specs=[
            pl.BlockSpec((tq, GQ * HEAD_DIM), lambda bi, kh, i: (bi * nq + i, kh)),
            pl.BlockSpec((t, HEAD_DIM), lambda bi, kh, i: (bi, kh)),
            pl.BlockSpec((t, HEAD_DIM), lambda bi, kh, i: (bi, N_KV_HEADS + kh)),
            pl.BlockSpec((None, GQ, 1, 2 * PAGE_SIZE), lambda bi, kh, i: (kh, 0, 0, 0)),
        ],
        out_specs=pl.BlockSpec((tq, GQ * HEAD_DIM), lambda bi, kh, i: (bi * nq + i, kh)),
        out_shape=jax.ShapeDtypeStruct((b * t, N_B_HEADS * HEAD_DIM), BF16),
        scratch_shapes=[pltpu.VMEM((t, HEAD_DIM), BF16), pltpu.VMEM((t, HEAD_DIM), BF16)],
        compiler_params=_params("arbitrary", "arbitrary", "arbitrary"),
        name="attn_b_prompt",
    )(q, kv, kv, bias)


def _attn_b_sample_body(pt_ref, q_ref, kvn_ref, bias_ref, *rest, s_new, n_steps):
    page_refs = rest[:PAGES_PER_STEP]
    o_ref, run_ref, acc_ref = rest[PAGES_PER_STEP:]
    p = pl.program_id(1)
    rows = GQ * s_new
    row_tok = lax.broadcasted_iota(jnp.int32, (rows, HEAD_DIM), 0) % s_new

    @pl.when(p == 0)
    def _():
        for kh in range(N_KV_HEADS):
            q = q_ref[kh]
            run = jnp.zeros((rows, HEAD_DIM), F32)
            acc = jnp.zeros((rows, HEAD_DIM), F32)
            for j in reversed(range(s_new)):
                kj = kvn_ref[0, kh, pl.ds(j, 1), :]
                vj = kvn_ref[1, kh, pl.ds(j, 1), :]
                z = jnp.sum(q * kj, axis=-1, keepdims=True) * SB_LOGIT_SCALE + bias_ref[kh]
                valid = row_tok > j
                sp = _softplus2(z)
                a = jnp.where(valid, jnp.exp2(z - sp + run), 0.0)
                acc = acc + a * vj
                run = run + jnp.where(valid, -sp, 0.0)
            run_ref[kh] = run
            acc_ref[kh] = acc

    suffix = _suffix_matrix()
    stride = 2 * N_KV_HEADS
    dn = (((1,), (1,)), ((), ()))
    def head_rows(first):
        return jnp.concatenate([r[pl.ds(first, PAGE_SIZE, stride=stride), :] for r in page_refs], axis=0).astype(BF16)

    def later_and_run(s, run):
        later = []
        for u in range(PAGES_PER_STEP):
            su = s[u * rows:(u + 1) * rows]
            later.append(su[:, :PAGE_SIZE] + run)
            run = run + su[:, PAGE_SIZE:]
        return jnp.concatenate(later, axis=1), run

    heads = range(N_KV_HEADS)
    zs = [lax.dot_general(q_ref[kh].astype(BF16), head_rows(kh), dn, preferred_element_type=F32) * SB_LOGIT_SCALE
          + jnp.tile(bias_ref[kh], (1, PAGES_PER_STEP)) for kh in heads]
    sps = [_softplus2(z) for z in zs]
    ss = [jnp.dot(jnp.concatenate([_hi_lo(-sp[:, u * PAGE_SIZE:(u + 1) * PAGE_SIZE])
                                   for u in range(PAGES_PER_STEP)], axis=0),
                  suffix, preferred_element_type=F32) for sp in sps]
    lrs = [later_and_run(s, run_ref[kh]) for kh, s in zip(heads, ss)]
    avals = [jnp.exp2(z - sp + later) for z, sp, (later, _) in zip(zs, sps, lrs)]
    pvs = [jnp.dot(a.astype(BF16), head_rows(N_KV_HEADS + kh), preferred_element_type=F32)
           for kh, a in zip(heads, avals)]
    for kh in heads:
        acc_ref[kh] += pvs[kh]
        run_ref[kh] = lrs[kh][1]

    @pl.when(p == n_steps - 1)
    def _():
        o_ref[...] = acc_ref[...]


def _attn_b_sample(q_rows, kv_new, sb, cache_kv, page_table, *, s_new):
    bs, n_pages = page_table.shape
    assert n_pages % PAGES_PER_STEP == 0
    n_steps = n_pages // PAGES_PER_STEP
    rows = GQ * s_new
    bias = jnp.broadcast_to(sb[:, :, None, None], (N_KV_HEADS, GQ, s_new, HEAD_DIM)).reshape(
        N_KV_HEADS, rows, HEAD_DIM)
    pool = cache_kv.reshape(cache_kv.shape[0], PAGE_SIZE * 2 * N_KV_HEADS, HEAD_DIM)

    def page_spec(u):
        return pl.BlockSpec((None, PAGE_SIZE * 2 * N_KV_HEADS, HEAD_DIM),
                            lambda b, p, pt: (pt[b, n_pages - 1 - (p * PAGES_PER_STEP + u)], 0, 0))

    grid_spec = pltpu.PrefetchScalarGridSpec(
        num_scalar_prefetch=1,
        grid=(bs, n_steps),
        in_specs=[
            pl.BlockSpec((None, N_KV_HEADS, rows, HEAD_DIM), lambda b, p, pt: (b, 0, 0, 0)),
            pl.BlockSpec((None, 2, N_KV_HEADS, s_new, HEAD_DIM), lambda b, p, pt: (b, 0, 0, 0, 0)),
            pl.BlockSpec((N_KV_HEADS, rows, HEAD_DIM), lambda b, p, pt: (0, 0, 0)),
        ] + [page_spec(u) for u in range(PAGES_PER_STEP)],
        out_specs=pl.BlockSpec((None, N_KV_HEADS, rows, HEAD_DIM), lambda b, p, pt: (b, 0, 0, 0)),
        scratch_shapes=[pltpu.VMEM((N_KV_HEADS, rows, HEAD_DIM), F32),
                        pltpu.VMEM((N_KV_HEADS, rows, HEAD_DIM), F32)],
    )
    return pl.pallas_call(
        functools.partial(_attn_b_sample_body, s_new=s_new, n_steps=n_steps),
        grid_spec=grid_spec,
        out_shape=jax.ShapeDtypeStruct((bs, N_KV_HEADS, rows, HEAD_DIM), F32),
        compiler_params=_params("arbitrary", "arbitrary"),
        name="attn_b_sample",
    )(page_table, q_rows, kv_new, bias, *([pool] * PAGES_PER_STEP))


def _mlp(x, gain, w_up, w_down, *, tiles, layer=0, emit_w=False):
    if emit_w:
        h, w_up_b = _norm_matmul(x, gain, w_up, mode="relu2", out_dtype=BF16, tiles=tiles, layer=layer, emit_w=True)
        y, w_down_b = _matmul_res(h, w_down, x, tiles=tiles, layer=layer, emit_w=True)
        return y, w_up_b, w_down_b
    (xn,) = _rms_norm(x, [gain], tr=tiles.tr_norm)
    h = _norm_matmul(xn, None, w_up, mode="relu2", out_dtype=BF16, tiles=tiles)
    return _matmul_res(h, w_down, x, tiles=tiles)


@jax.jit
def _step(x_prompt, x_sample, cache_win_g0, cache_win_g1, cache_win_g2, cache_kv, page_table, bias_table,
          norm_mix, norm_mlp, w_a_in, g_a_q, g_a_k, w_a_out, norm_kv, w_kv, w_b_q, b_sb, w_b_out, w_up, w_down):
    bp, tp, _ = x_prompt.shape
    bs, ts, _ = x_sample.shape
    pt, st = PROMPT_TILES, _sample_tiles(bs * ts)
    xp = x_prompt.reshape(bp * tp, D_MODEL)
    xs = x_sample.reshape(bs * ts, D_MODEL)
    caches = (cache_win_g0[0], cache_win_g1[0], cache_win_g2[0])

    head_gain = jnp.concatenate([jnp.tile(g_a_q[0], N_A_HEADS), jnp.tile(g_a_k[0], N_A_HEADS),
                                 jnp.ones((A_WIDTH,), F32)])
    col_gain = jnp.tile(head_gain, N_GROUPS).reshape(1, -1).astype(F32)
    qkv_s, w_in = _norm_matmul(xs, norm_mix[0], w_a_in, mode="qkv", out_dtype=F32, tiles=st,
                               col_gain=col_gain, emit_w=True)
    (xn_p,) = _rms_norm(xp, [norm_mix[0]], tr=pt.tr_norm)
    qkv_p = _norm_matmul(xn_p, None, w_in, mode="qkv", out_dtype=F32, tiles=pt, col_gain=col_gain)

    bias_m = _bias_by_stride(bias_table)
    qkv_s6 = qkv_s.reshape(bs, ts, N_GROUPS, 3, N_A_HEADS, HEAD_DIM)
    o_s = _attn_a_sample(qkv_s6, caches, bias_m, bs=bs, s_new=ts)
    o_s = o_s.reshape(bs * ts, A_WIDTH).astype(BF16)
    o_p = _attn_a_prompt(qkv_p, _prompt_bias(bias_m), b=bp, t=tp)

    xs, w_ao = _matmul_res(o_s, w_a_out, xs, tiles=st, emit_w=True)
    xp = _matmul_res(o_p, w_ao, xp, tiles=pt)
    xs, w_up0, w_dn0 = _mlp(xs, norm_mlp[0], w_up, w_down, tiles=st, layer=0, emit_w=True)
    xp = _mlp(xp, norm_mlp[0], w_up0, w_dn0, tiles=pt)

    kv_s, w_kvb = _norm_matmul(xs, norm_kv, w_kv, mode="plain", out_dtype=F32, tiles=st, emit_w=True)
    xn_kv, xn_bq = _rms_norm(xp, [norm_kv, norm_mix[1]], tr=pt.tr_norm)
    kv_p = _norm_matmul(xn_kv, None, w_kvb, mode="plain", out_dtype=F32, tiles=pt)

    qb_s, w_bq = _norm_matmul(xs, norm_mix[1], w_b_q, mode="plain", out_dtype=F32, tiles=st, emit_w=True)
    qb_p = _norm_matmul(xn_bq, None, w_bq, mode="plain", out_dtype=BF16, tiles=pt)

    sb = b_sb[0].astype(F32).reshape(N_KV_HEADS, GQ) * LOG2E
    q_rows = qb_s.reshape(bs, ts, N_KV_HEADS, GQ, HEAD_DIM).transpose(0, 2, 3, 1, 4).reshape(
        bs, N_KV_HEADS, GQ * ts, HEAD_DIM)
    kv_s5 = kv_s.reshape(bs, ts, 2, N_KV_HEADS, HEAD_DIM)
    ob_s = _attn_b_sample(q_rows, kv_s5.transpose(0, 2, 3, 1, 4), sb, cache_kv, page_table, s_new=ts)
    ob_s = ob_s.reshape(bs, N_KV_HEADS, GQ, ts, HEAD_DIM).transpose(0, 3, 1, 2, 4).reshape(
        bs * ts, N_B_HEADS * HEAD_DIM).astype(BF16)
    ob_p = _attn_b_prompt(qb_p, kv_p, sb, b=bp, t=tp)

    xs, w_bo = _matmul_res(ob_s, w_b_out, xs, tiles=st, emit_w=True)
    xp = _matmul_res(ob_p, w_bo, xp, tiles=pt)
    xs, w_up1, w_dn1 = _mlp(xs, norm_mlp[1], w_up, w_down, tiles=st, layer=1, emit_w=True)
    xp = _mlp(xp, norm_mlp[1], w_up1, w_dn1, tiles=pt)

    qkv_p6 = qkv_p.reshape(bp, tp, N_GROUPS, 3, N_A_HEADS, HEAD_DIM)
    win_p, win_s = [], []
    for g, (window, _) in enumerate(DILATED_PAIRS):
        wp = min(window, tp)
        win_p.append(qkv_p6[:, tp - wp:, g, 1:3][None])
        win_s.append(jnp.concatenate([caches[g][:, ts:], qkv_s6[:, :, g, 1:3]], axis=1)[None])
    return (xp.reshape(bp, tp, D_MODEL), xs.reshape(bs, ts, D_MODEL), *win_p, *win_s,
            kv_p.reshape(bp, tp, 2, N_KV_HEADS, HEAD_DIM), kv_s5)


def kernel(x_prompt, x_sample, cache_win_g0, cache_win_g1, cache_win_g2, cache_kv, page_table, bias_table,
           norm_mix, norm_mlp, w_a_in, g_a_q, g_a_k, w_a_out, norm_kv, w_kv, w_b_q, b_sb, w_b_out, w_up, w_down):
    return _step(x_prompt, x_sample, cache_win_g0, cache_win_g1, cache_win_g2, cache_kv, page_table, bias_table,
                 norm_mix, norm_mlp, w_a_in, g_a_q, g_a_k, w_a_out, norm_kv, w_kv, w_b_q, b_sb, w_b_out,
                 w_up, w_down)
```

```python
import functools
import math
from typing import NamedTuple

import jax
import jax.numpy as jnp
from jax import lax
from jax.experimental import pallas as pl
from jax.experimental.pallas import tpu as pltpu

F32 = jnp.float32
BF16 = jnp.bfloat16

D_MODEL = 4096
HEAD_DIM = 128
DILATED_PAIRS = ((128, 1), (512, 4), (2048, 16))
N_GROUPS = len(DILATED_PAIRS)
N_A_HEADS = D_MODEL // (2 * HEAD_DIM)
A_WIDTH = N_A_HEADS * HEAD_DIM
N_B_HEADS = D_MODEL // HEAD_DIM
N_KV_HEADS = 8
GQ = N_B_HEADS // N_KV_HEADS
N_BUCKETS = 32
MAX_DISTANCE = 2048
RMS_EPS = 1e-6
PAGE_SIZE = 128
BAND = 128
ATT_SCALE = HEAD_DIM ** -0.5
MASKED = -1e30

V7X_VMEM_LIMIT_BYTES = 60 * 1024 * 1024
V7X_MXU_COLS = 256
NORM_ROWS = 16
PAGES_PER_STEP = 4
KEY_BLOCKS_PER_SWEEP = 3


class Tiles(NamedTuple):
    tm: int
    tn_full_k: int
    tn_res: int
    tk_res: int
    tr_norm: int


PROMPT_TILES = Tiles(tm=1024, tn_full_k=1024, tn_res=1024, tk_res=4096, tr_norm=512)


def _sample_tiles(rows):
    return Tiles(tm=rows, tn_full_k=512, tn_res=1024, tk_res=2048, tr_norm=rows)


def _params(*semantics):
    return pltpu.CompilerParams(dimension_semantics=semantics,
                                vmem_limit_bytes=V7X_VMEM_LIMIT_BYTES)


def _weight_spec(w, layer, block, index):
    if w.ndim == 2:
        return pl.BlockSpec(block, index)
    return pl.BlockSpec((None,) + block, lambda *g: (layer,) + index(*g))


def _head_rms(t, cg):
    ms = jnp.mean(t * t, axis=-1, keepdims=True)
    return t * lax.rsqrt(ms + RMS_EPS) * cg


def _rms_rows(x_ref, g_refs, o_refs, n_rows):
    per = math.gcd(n_rows // NORM_ROWS, 4)

    def rows_step(c, carry):
        groups = [pl.ds(pl.multiple_of((c * per + u) * NORM_ROWS, NORM_ROWS), NORM_ROWS) for u in range(per)]
        scales = [lax.rsqrt(jnp.mean(jnp.square(x_ref[rows, :]), axis=-1, keepdims=True) + RMS_EPS)
                  for rows in groups]
        for rows, scale in zip(groups, scales):
            y = x_ref[rows, :] * scale
            for g_ref, o_ref in zip(g_refs, o_refs):
                o_ref[rows, :] = (y * g_ref[...]).astype(BF16)
        return carry

    lax.fori_loop(0, n_rows // (NORM_ROWS * per), rows_step, 0)


def _rms_norm_body(x_ref, *refs, n_gains, tr):
    _rms_rows(x_ref, refs[:n_gains], refs[n_gains:], tr)


def _rms_norm(x, gains, *, tr):
    m, k = x.shape
    n = len(gains)
    outs = pl.pallas_call(
        functools.partial(_rms_norm_body, n_gains=n, tr=tr),
        grid=(m // tr,),
        in_specs=[pl.BlockSpec((tr, k), lambda i: (i, 0))] + [pl.BlockSpec((1, k), lambda i: (0, 0))] * n,
        out_specs=[pl.BlockSpec((tr, k), lambda i: (i, 0))] * n,
        out_shape=[jax.ShapeDtypeStruct((m, k), BF16)] * n,
        compiler_params=_params("arbitrary"),
        name=f"rms_norm_{n}",
    )(x, *[g.reshape(1, k) for g in gains])
    return outs


def _norm_matmul_body(x_ref, *refs, mode, tm, tn, emit_w, fused_norm):
    if fused_norm:
        g_ref, w_ref, cg_ref, o_ref, *rest = refs
    else:
        w_ref, cg_ref, o_ref, *rest = refs
    if emit_w:
        wb_ref, *rest = rest
    j = pl.program_id(1)

    if fused_norm:
        (xn_ref,) = rest

        @pl.when(j == 0)
        def _():
            _rms_rows(x_ref, [g_ref], [xn_ref], tm)
    else:
        xn_ref = x_ref

    if emit_w:
        wb_ref[...] = w_ref[...].astype(BF16)
        w_src = wb_ref
    else:
        w_src = w_ref

    if mode == "relu2":
        r = jnp.maximum(jnp.dot(xn_ref[...], w_src[...], preferred_element_type=F32), 0.0)
        o_ref[...] = (r * r).astype(o_ref.dtype)
    elif mode == "plain":
        o_ref[...] = jnp.dot(xn_ref[...], w_src[...], preferred_element_type=F32).astype(o_ref.dtype)
    else:
        kind = (j * tn // A_WIDTH) % 3

        @pl.when(kind == 2)
        def _():
            o_ref[...] = jnp.dot(xn_ref[...], w_src[...], preferred_element_type=F32)

        @pl.when(kind != 2)
        def _():
            for c0 in range(0, tn, V7X_MXU_COLS):
                y = jnp.dot(xn_ref[...], w_src[:, c0:c0 + V7X_MXU_COLS], preferred_element_type=F32)
                for c in range(0, V7X_MXU_COLS, HEAD_DIM):
                    cols = slice(c0 + c, c0 + c + HEAD_DIM)
                    o_ref[:, cols] = _head_rms(y[:, c:c + HEAD_DIM], cg_ref[:, cols])


def _norm_matmul(x, gain, w, *, mode, out_dtype, tiles, layer=0, col_gain=None, emit_w=False):
    m, k = x.shape
    n = w.shape[-1]
    tm, tn = tiles.tm, tiles.tn_full_k
    fused_norm = gain is not None
    assert not emit_w or m == tm
    assert fused_norm or x.dtype == BF16
    if col_gain is None:
        col_gain = jnp.ones((1, n), F32)
    body = functools.partial(_norm_matmul_body, mode=mode, tm=tm, tn=tn, emit_w=emit_w, fused_norm=fused_norm)
    out_specs = pl.BlockSpec((tm, tn), lambda i, j: (i, j))
    out_shape = jax.ShapeDtypeStruct((m, n), out_dtype)
    if emit_w:
        out_specs = [out_specs, pl.BlockSpec((k, tn), lambda i, j: (0, j))]
        out_shape = [out_shape, jax.ShapeDtypeStruct((k, n), BF16)]
    if fused_norm:
        x_specs = [pl.BlockSpec((tm, k), lambda i, j: (i, 0), pipeline_mode=pl.Buffered(1)),
                   pl.BlockSpec((1, k), lambda i, j: (0, 0))]
        x_args = (x, gain.reshape(1, k))
        scratch = [pltpu.VMEM((tm, k), BF16)]
    else:
        x_specs = [pl.BlockSpec((tm, k), lambda i, j: (i, 0))]
        x_args = (x,)
        scratch = []
    return pl.pallas_call(
        body,
        grid=(m // tm, n // tn),
        in_specs=x_specs + [
            _weight_spec(w, layer, (k, tn), lambda i, j: (0, j)),
            pl.BlockSpec((1, tn), lambda i, j: (0, j)),
        ],
        out_specs=out_specs,
        out_shape=out_shape,
        scratch_shapes=scratch,
        compiler_params=_params("arbitrary", "arbitrary"),
        name=f"norm_matmul_{mode}_{tm}",
    )(*x_args, w, col_gain)


def _matmul_res_body(a_ref, w_ref, r_ref, o_ref, *rest, emit_w):
    kk = pl.program_id(2)
    if emit_w:
        (wb_ref,) = rest
        wb_ref[...] = w_ref[...].astype(BF16)
        w_src = wb_ref
    else:
        w_src = w_ref
    d = jnp.dot(a_ref[...], w_src[...], preferred_element_type=F32)

    @pl.when(kk == 0)
    def _():
        o_ref[...] = r_ref[...] + d

    @pl.when(kk != 0)
    def _():
        o_ref[...] += d


def _matmul_res(a, w, res, *, tiles, layer=0, emit_w=False):
    m, k = a.shape
    n = w.shape[-1]
    tm, tn, tk = tiles.tm, tiles.tn_res, min(tiles.tk_res, k)
    assert not emit_w or m == tm
    out_specs = pl.BlockSpec((tm, tn), lambda i, j, kk: (i, j))
    out_shape = jax.ShapeDtypeStruct((m, n), F32)
    if emit_w:
        out_specs = [out_specs, pl.BlockSpec((tk, tn), lambda i, j, kk: (kk, j))]
        out_shape = [out_shape, jax.ShapeDtypeStruct((k, n), BF16)]
    return pl.pallas_call(
        functools.partial(_matmul_res_body, emit_w=emit_w),
        grid=(m // tm, n // tn, k // tk),
        in_specs=[
            pl.BlockSpec((tm, tk), lambda i, j, kk: (i, kk)),
            _weight_spec(w, layer, (tk, tn), lambda i, j, kk: (kk, j)),
            pl.BlockSpec((tm, tn), lambda i, j, kk: (i, j)),
        ],
        out_specs=out_specs,
        out_shape=out_shape,
        compiler_params=_params("arbitrary", "arbitrary", "arbitrary"),
        name=f"matmul_res_{tm}_{k}",
    )(a, w, res)


def _rel_bucket(dist):
    max_exact = N_BUCKETS // 2
    d32 = jnp.maximum(dist, 1).astype(F32)
    large = max_exact + (jnp.log(d32 / max_exact) / math.log(MAX_DISTANCE / max_exact)
                         * (N_BUCKETS - max_exact)).astype(jnp.int32)
    large = jnp.minimum(large, N_BUCKETS - 1)
    return jnp.where(dist < max_exact, dist, large)


def _bias_by_stride(bias_table):
    m = jnp.arange(BAND + 1)
    return jnp.stack([bias_table[:, g][_rel_bucket(m * d)] for g, (_, d) in enumerate(DILATED_PAIRS)])


def _prompt_bias(bias_m):
    period = 2 * BAND + 1
    u = jnp.concatenate([bias_m[:, ::-1, :], jnp.full((N_GROUPS, period - BAND - 1, N_A_HEADS), MASKED, F32)],
                        axis=1).transpose(0, 2, 1)
    flat = jnp.tile(u, (1, 1, BAND))[:, :, :BAND * 2 * BAND]
    return flat.reshape(N_GROUPS, N_A_HEADS, BAND, 2 * BAND)


def _attn_a_prompt_body(q0, k0, v0, q1, k1, v1, q2, k2, v2, bias_ref, o_ref, og_ref, lse_ref, *, t):
    qkv = ((q0, k0, v0), (q1, k1, v1), (q2, k2, v2))
    ones = jnp.ones((BAND, HEAD_DIM), BF16)
    dn = (((1,), (1,)), ((), ()))

    def rows_of(start, d):
        if d == 1:
            return pl.ds(start, BAND)
        return pl.ds(start, BAND, stride=d)

    def blocks(g, specs):
        q_ref, k_ref, v_ref = qkv[g]
        with_prev = specs[0][1] is not None

        def logits(q, rows, bias):
            return lax.dot_general(q, k_ref[rows, :].astype(BF16), dn, preferred_element_type=F32) * ATT_SCALE + bias

        def weighted(p, rows):
            vx = jnp.concatenate([v_ref[rows, :].astype(BF16), ones], axis=1)
            return jnp.dot(p.astype(BF16), vx, preferred_element_type=F32)

        qs = [q_ref[cur, :].astype(BF16) for cur, _ in specs]
        s_cs = [logits(q, cur, bias_ref[g, :, BAND:]) for q, (cur, _) in zip(qs, specs)]
        mxs = [jnp.max(s, axis=-1, keepdims=True) for s in s_cs]
        if with_prev:
            s_ps = [logits(q, prev, bias_ref[g, :, :BAND]) for q, (_, prev) in zip(qs, specs)]
            mxs = [jnp.maximum(mx, jnp.max(s, axis=-1, keepdims=True)) for mx, s in zip(mxs, s_ps)]
        accs = [weighted(jnp.exp(s - mx), cur) for s, mx, (cur, _) in zip(s_cs, mxs, specs)]
        if with_prev:
            accs = [acc + weighted(jnp.exp(s - mx), prev) for acc, s, mx, (_, prev) in zip(accs, s_ps, mxs, specs)]
        for acc, mx, (cur, _) in zip(accs, mxs, specs):
            den = acc[:, HEAD_DIM:]
            og_ref[g, cur, :] = acc[:, :HEAD_DIM] / den
            lse_ref[g, cur, :] = mx + jnp.log(den)

    for g, (_, d) in enumerate(DILATED_PAIRS):
        span = BAND * d
        n_blk = t // span
        if n_blk == 1:
            per = 4

            def residue_step(it, carry, g=g, d=d, per=per):
                blocks(g, [(rows_of(it * per + u, d), None) for u in range(per)])
                return carry
            lax.fori_loop(0, d // per, residue_step, 0)
        elif d == 1:
            per = 5
            assert (n_blk - 1) % per == 0
            blocks(g, [(rows_of(0, d), None)])

            def blk_step(it, carry, g=g, d=d, per=per, span=span):
                starts = [pl.multiple_of((1 + it * per + u) * span, BAND) for u in range(per)]
                blocks(g, [(rows_of(st, d), rows_of(st - span, d)) for st in starts])
                return carry
            lax.fori_loop(0, (n_blk - 1) // per, blk_step, 0)
        else:
            blocks(g, [(rows_of(r, d), None) for r in range(d)])

            def blk_step(n, carry, g=g, d=d, span=span):
                blocks(g, [(rows_of(n * span + r, d), rows_of(n * span + r - span, d)) for r in range(d)])
                return carry
            lax.fori_loop(1, n_blk, blk_step, 0)

    def merge_step(c, carry):
        rows = pl.ds(pl.multiple_of(c * BAND, BAND), BAND)
        l0, l1, l2 = lse_ref[0, rows, :], lse_ref[1, rows, :], lse_ref[2, rows, :]
        mx = jnp.maximum(jnp.maximum(l0, l1), l2)
        e0, e1, e2 = jnp.exp(l0 - mx), jnp.exp(l1 - mx), jnp.exp(l2 - mx)
        tot = e0 + e1 + e2
        o = (e0 / tot) * og_ref[0, rows, :] + (e1 / tot) * og_ref[1, rows, :] + (e2 / tot) * og_ref[2, rows, :]
        o_ref[rows, :] = o.astype(o_ref.dtype)
        return carry

    lax.fori_loop(0, t // BAND, merge_step, 0)


def _attn_a_prompt(qkv, bias, *, b, t):
    heads = N_A_HEADS

    def col_spec(g, c):
        base = (g * 3 + c) * heads
        return pl.BlockSpec((t, HEAD_DIM), lambda bi, h: (bi, base + h))

    in_specs = [col_spec(g, c) for g in range(N_GROUPS) for c in range(3)]
    in_specs.append(pl.BlockSpec((N_GROUPS, None, BAND, 2 * BAND), lambda bi, h: (0, h, 0, 0)))
    return pl.pallas_call(
        functools.partial(_attn_a_prompt_body, t=t),
        grid=(b, heads),
        in_specs=in_specs,
        out_specs=pl.BlockSpec((t, HEAD_DIM), lambda bi, h: (bi, h)),
        out_shape=jax.ShapeDtypeStruct((b * t, A_WIDTH), BF16),
        scratch_shapes=[pltpu.VMEM((N_GROUPS, t, HEAD_DIM), F32),
                        pltpu.VMEM((N_GROUPS, t, HEAD_DIM), F32)],
        compiler_params=_params("arbitrary", "arbitrary"),
        name="attn_a_prompt",
    )(*([qkv] * 9), bias)


def _attn_a_sample_body(qkv_ref, c0_ref, c1_ref, c2_ref, bb0_ref, bb12_ref, bn0_ref, bn12_ref, o_ref):
    s = pl.program_id(1)
    outs, lses = [], []
    for g in range(N_GROUPS):
        q = qkv_ref[s, g, 0]
        if g == 0:
            kb, vb, bias_b = c0_ref[:, 0], c0_ref[:, 1], bb0_ref[...]
            kn, vn, bias_n = qkv_ref[:, 0, 1], qkv_ref[:, 0, 2], bn0_ref[...]
        else:
            c_ref = c1_ref if g == 1 else c2_ref
            kb, vb, bias_b = c_ref[:, 0], c_ref[:, 1], bb12_ref[g - 1]
            kn, vn, bias_n = qkv_ref[pl.ds(s, 1), g, 1], qkv_ref[pl.ds(s, 1), g, 2], bn12_ref[g - 1][None]
        lb = jnp.sum(q[None] * kb, axis=-1, keepdims=True) * ATT_SCALE + bias_b
        ln = jnp.sum(q[None] * kn, axis=-1, keepdims=True) * ATT_SCALE + bias_n
        mx = jnp.maximum(jnp.max(lb, axis=0), jnp.max(ln, axis=0))
        pb = jnp.exp(lb - mx[None])
        pn = jnp.exp(ln - mx[None])
        den = jnp.sum(pb, axis=0) + jnp.sum(pn, axis=0)
        acc = jnp.sum(pb * vb, axis=0) + jnp.sum(pn * vn, axis=0)
        outs.append(acc / den)
        lses.append(mx + jnp.log(den))
    mx = jnp.maximum(jnp.maximum(lses[0], lses[1]), lses[2])
    es = [jnp.exp(l - mx) for l in lses]
    tot = es[0] + es[1] + es[2]
    o_ref[...] = (es[0] / tot) * outs[0] + (es[1] / tot) * outs[1] + (es[2] / tot) * outs[2]


def _attn_a_sample(qkv_s, caches, bias_m, *, bs, s_new):
    h = N_A_HEADS
    lane = (h, HEAD_DIM)

    def dense(x):
        return jnp.broadcast_to(x[..., None], x.shape + (HEAD_DIM,)).astype(F32)

    sq = jnp.arange(s_new)
    r = jnp.arange(BAND)
    m_b = BAND + sq[:, None] - r[None, :]
    bb0 = dense(jnp.where((m_b <= BAND)[..., None], bias_m[0][jnp.clip(m_b, 0, BAND)], MASKED))
    m_n = sq[:, None] - sq[None, :]
    bn0 = dense(jnp.where((m_n >= 0)[..., None], bias_m[0][jnp.clip(m_n, 0, BAND)], MASKED))
    bb12 = dense(jnp.stack([bias_m[g][BAND - r] for g in (1, 2)]))
    bn12 = dense(jnp.stack([bias_m[g][0] for g in (1, 2)]))

    c0 = caches[0]
    c1 = caches[1].reshape(bs, BAND, DILATED_PAIRS[1][1], 2, h, HEAD_DIM)
    c2 = caches[2].reshape(bs, BAND, DILATED_PAIRS[2][1], 2, h, HEAD_DIM)
    return pl.pallas_call(
        _attn_a_sample_body,
        grid=(bs, s_new),
        in_specs=[
            pl.BlockSpec((None, s_new, N_GROUPS, 3) + lane, lambda b, s: (b, 0, 0, 0, 0, 0)),
            pl.BlockSpec((None, BAND, 2) + lane, lambda b, s: (b, 0, 0, 0, 0)),
            pl.BlockSpec((None, BAND, None, 2) + lane, lambda b, s: (b, 0, s, 0, 0, 0)),
            pl.BlockSpec((None, BAND, None, 2) + lane, lambda b, s: (b, 0, s, 0, 0, 0)),
            pl.BlockSpec((None, BAND) + lane, lambda b, s: (s, 0, 0, 0)),
            pl.BlockSpec((2, BAND) + lane, lambda b, s: (0, 0, 0, 0)),
            pl.BlockSpec((None, s_new) + lane, lambda b, s: (s, 0, 0, 0)),
            pl.BlockSpec((2,) + lane, lambda b, s: (0, 0, 0)),
        ],
        out_specs=pl.BlockSpec((None, None) + lane, lambda b, s: (b, s, 0, 0)),
        out_shape=jax.ShapeDtypeStruct((bs, s_new) + lane, F32),
        compiler_params=_params("arbitrary", "arbitrary"),
        name="attn_a_sample",
    )(qkv_s, c0, c1, c2, bb0, bb12, bn0, bn12)


LOG2E = 1.0 / math.log(2.0)
SB_LOGIT_SCALE = ATT_SCALE * LOG2E


def _softplus2(z2):
    return jnp.maximum(z2, 0.0) + jnp.log2(1.0 + jnp.exp2(-jnp.abs(z2)))


def _suffix_matrix():
    n = 2 * PAGE_SIZE
    j = lax.broadcasted_iota(jnp.int32, (n, n), 0) % PAGE_SIZE
    c = lax.broadcasted_iota(jnp.int32, (n, n), 1)
    return jnp.where((c >= PAGE_SIZE) | (j > c), 1.0, 0.0).astype(BF16)


def _hi_lo(l):
    hi = l.astype(BF16)
    lo = (l - hi.astype(F32)).astype(BF16)
    return jnp.concatenate([hi, lo], axis=1)


def _attn_b_prompt_body(q_ref, k_ref, v_ref, bias_ref, o_ref, kb_ref, vb_ref, *, t, tq):
    i = pl.program_id(2)

    @pl.when(i == 0)
    def _():
        def cast_step(c, carry):
            rows = pl.ds(pl.multiple_of(c * 256, 256), 256)
            kb_ref[rows, :] = (k_ref[rows, :] * SB_LOGIT_SCALE).astype(BF16)
            vb_ref[rows, :] = v_ref[rows, :].astype(BF16)
            return carry
        lax.fori_loop(0, t // 256, cast_step, 0)

    suffix = _suffix_matrix()
    dn = (((1,), (1,)), ((), ()))
    heads = range(GQ)

    def block_rows(jlast, nblk):
        return pl.ds(pl.multiple_of((jlast - (nblk - 1)) * PAGE_SIZE, PAGE_SIZE), nblk * PAGE_SIZE)

    def logits(rows, nblk):
        k = kb_ref[rows, :]
        return [lax.dot_general(q_ref[:, g * HEAD_DIM:(g + 1) * HEAD_DIM], k, dn, preferred_element_type=F32)
                + bias_ref[g, :, :nblk * PAGE_SIZE] for g in heads]

    def suffix_sums(zs, nblk, visible):
        sps = [_softplus2(z) for z in zs]
        ls = [-sp if visible is None else jnp.where(visible, -sp, 0.0) for sp in sps]
        ss = [jnp.dot(jnp.concatenate([_hi_lo(l[:, u * PAGE_SIZE:(u + 1) * PAGE_SIZE])
                                       for u in reversed(range(nblk))], axis=0),
                      suffix, preferred_element_type=F32) for l in ls]
        return sps, ss

    def weights(zs, sps, ss, runs, nblk, visible):
        avals, new_runs = [], []
        for g, (z, sp, s) in enumerate(zip(zs, sps, ss)):
            run = None if runs is None else runs[g]
            later = [None] * nblk
            for idx in range(nblk):
                su = s[idx * tq:(idx + 1) * tq]
                later[nblk - 1 - idx] = su[:, :PAGE_SIZE] if run is None else su[:, :PAGE_SIZE] + run
                run = su[:, PAGE_SIZE:] if run is None else run + su[:, PAGE_SIZE:]
            a = jnp.exp2(z - sp + (later[0] if nblk == 1 else jnp.concatenate(later, axis=1)))
            if visible is not None:
                a = jnp.where(visible, a, 0.0)
            avals.append(a.astype(BF16))
            new_runs.append(run)
        return avals, tuple(new_runs)

    def weighted_values(avals, rows):
        v = vb_ref[rows, :]
        return [jnp.dot(a, v, preferred_element_type=F32) for a in avals]

    def sweep(jlast, nblk, state):
        rows = block_rows(jlast, nblk)
        visible = None
        if state is None:
            qpos = lax.broadcasted_iota(jnp.int32, (tq, PAGE_SIZE), 0)
            kpos = lax.broadcasted_iota(jnp.int32, (tq, PAGE_SIZE), 1)
            visible = kpos < qpos
        zs = logits(rows, nblk)
        sps, ss = suffix_sums(zs, nblk, visible)
        avals, runs = weights(zs, sps, ss, None if state is None else state[0], nblk, visible)
        pvs = weighted_values(avals, rows)
        accs = pvs if state is None else [acc + pv for acc, pv in zip(state[1], pvs)]
        return runs, tuple(accs)

    n_groups = lax.shift_right_logical(i * 11, 5)
    rem = i - n_groups * KEY_BLOCKS_PER_SWEEP
    state = sweep(i, 1, None)
    state = lax.fori_loop(0, rem, lambda u, st: sweep(i - 1 - u, 1, st), state)
    first = i - 1 - rem
    _, accs = lax.fori_loop(0, n_groups,
                            lambda p, st: sweep(first - KEY_BLOCKS_PER_SWEEP * p, KEY_BLOCKS_PER_SWEEP, st), state)
    for g in heads:
        o_ref[:, g * HEAD_DIM:(g + 1) * HEAD_DIM] = accs[g].astype(o_ref.dtype)


def _attn_b_prompt(q, kv, sb, *, b, t):
    tq = PAGE_SIZE
    nq = t // tq
    assert KEY_BLOCKS_PER_SWEEP == 3 and nq <= 16
    bias = jnp.broadcast_to(sb[:, :, None, None], (N_KV_HEADS, GQ, 1, KEY_BLOCKS_PER_SWEEP * PAGE_SIZE))
    return pl.pallas_call(
        functools.partial(_attn_b_prompt_body, t=t, tq=tq),
        grid=(b, N_KV_HEADS, nq),
        in_specs=[
            pl.BlockSpec((tq, GQ * HEAD_DIM), lambda bi, kh, i: (bi * nq + i, kh)),
            pl.BlockSpec((t, HEAD_DIM), lambda bi, kh, i: (bi, kh)),
            pl.BlockSpec((t, HEAD_DIM), lambda bi, kh, i: (bi, N_KV_HEADS + kh)),
            pl.BlockSpec((None, GQ, 1, KEY_BLOCKS_PER_SWEEP * PAGE_SIZE), lambda bi, kh, i: (kh, 0, 0, 0)),
        ],
        out_specs=pl.BlockSpec((tq, GQ * HEAD_DIM), lambda bi, kh, i: (bi * nq + i, kh)),
        out_shape=jax.ShapeDtypeStruct((b * t, N_B_HEADS * HEAD_DIM), BF16),
        scratch_shapes=[pltpu.VMEM((t, HEAD_DIM), BF16), pltpu.VMEM((t, HEAD_DIM), BF16)],
        compiler_params=_params("arbitrary", "arbitrary", "arbitrary"),
        name="attn_b_prompt",
    )(q, kv, kv, bias)


def _attn_b_sample_body(pt_ref, q_ref, kvn_ref, bias_ref, *rest, s_new, n_steps):
    page_refs = rest[:PAGES_PER_STEP]
    o_ref, run_ref, acc_ref = rest[PAGES_PER_STEP:]
    p = pl.program_id(1)
    rows = GQ * s_new
    row_tok = lax.broadcasted_iota(jnp.int32, (rows, HEAD_DIM), 0) % s_new

    @pl.when(p == 0)
    def _():
        for kh in range(N_KV_HEADS):
            q = q_ref[kh]
            run = jnp.zeros((rows, HEAD_DIM), F32)
            acc = jnp.zeros((rows, HEAD_DIM), F32)
            for j in reversed(range(s_new)):
                kj = kvn_ref[0, kh, pl.ds(j, 1), :]
                vj = kvn_ref[1, kh, pl.ds(j, 1), :]
                z = jnp.sum(q * kj, axis=-1, keepdims=True) * SB_LOGIT_SCALE + bias_ref[kh]
                valid = row_tok > j
                sp = _softplus2(z)
                a = jnp.where(valid, jnp.exp2(z - sp + run), 0.0)
                acc = acc + a * vj
                run = run + jnp.where(valid, -sp, 0.0)
            run_ref[kh] = run
            acc_ref[kh] = acc

    suffix = _suffix_matrix()
    stride = 2 * N_KV_HEADS
    dn = (((1,), (1,)), ((), ()))
    def head_rows(first):
        return jnp.concatenate([r[pl.ds(first, PAGE_SIZE, stride=stride), :] for r in page_refs], axis=0).astype(BF16)

    def later_and_run(s, run):
        later = []
        for u in range(PAGES_PER_STEP):
            su = s[u * rows:(u + 1) * rows]
            later.append(su[:, :PAGE_SIZE] + run)
            run = run + su[:, PAGE_SIZE:]
        return jnp.concatenate(later, axis=1), run

    heads = range(N_KV_HEADS)
    zs = [lax.dot_general(q_ref[kh].astype(BF16), head_rows(kh), dn, preferred_element_type=F32) * SB_LOGIT_SCALE
          + jnp.tile(bias_ref[kh], (1, PAGES_PER_STEP)) for kh in heads]
    sps = [_softplus2(z) for z in zs]
    ss = [jnp.dot(jnp.concatenate([_hi_lo(-sp[:, u * PAGE_SIZE:(u + 1) * PAGE_SIZE])
                                   for u in range(PAGES_PER_STEP)], axis=0),
                  suffix, preferred_element_type=F32) for sp in sps]
    lrs = [later_and_run(s, run_ref[kh]) for kh, s in zip(heads, ss)]
    avals = [jnp.exp2(z - sp + later) for z, sp, (later, _) in zip(zs, sps, lrs)]
    pvs = [jnp.dot(a.astype(BF16), head_rows(N_KV_HEADS + kh), preferred_element_type=F32)
           for kh, a in zip(heads, avals)]
    for kh in heads:
        acc_ref[kh] += pvs[kh]
        run_ref[kh] = lrs[kh][1]

    @pl.when(p == n_steps - 1)
    def _():
        o_ref[...] = acc_ref[...]


def _attn_b_sample(q_rows, kv_new, sb, cache_kv, page_table, *, s_new):
    bs, n_pages = page_table.shape
    assert n_pages % PAGES_PER_STEP == 0
    n_steps = n_pages // PAGES_PER_STEP
    rows = GQ * s_new
    bias = jnp.broadcast_to(sb[:, :, None, None], (N_KV_HEADS, GQ, s_new, HEAD_DIM)).reshape(
        N_KV_HEADS, rows, HEAD_DIM)
    pool = cache_kv.reshape(cache_kv.shape[0], PAGE_SIZE * 2 * N_KV_HEADS, HEAD_DIM)

    def page_spec(u):
        return pl.BlockSpec((None, PAGE_SIZE * 2 * N_KV_HEADS, HEAD_DIM),
                            lambda b, p, pt: (pt[b, n_pages - 1 - (p * PAGES_PER_STEP + u)], 0, 0))

    grid_spec = pltpu.PrefetchScalarGridSpec(
        num_scalar_prefetch=1,
        grid=(bs, n_steps),
        in_specs=[
            pl.BlockSpec((None, N_KV_HEADS, rows, HEAD_DIM), lambda b, p, pt: (b, 0, 0, 0)),
            pl.BlockSpec((None, 2, N_KV_HEADS, s_new, HEAD_DIM), lambda b, p, pt: (b, 0, 0, 0, 0)),
            pl.BlockSpec((N_KV_HEADS, rows, HEAD_DIM), lambda b, p, pt: (0, 0, 0)),
        ] + [page_spec(u) for u in range(PAGES_PER_STEP)],
        out_specs=pl.BlockSpec((None, N_KV_HEADS, rows, HEAD_DIM), lambda b, p, pt: (b, 0, 0, 0)),
        scratch_shapes=[pltpu.VMEM((N_KV_HEADS, rows, HEAD_DIM), F32),
                        pltpu.VMEM((N_KV_HEADS, rows, HEAD_DIM), F32)],
    )
    return pl.pallas_call(
        functools.partial(_attn_b_sample_body, s_new=s_new, n_steps=n_steps),
        grid_spec=grid_spec,
        out_shape=jax.ShapeDtypeStruct((bs, N_KV_HEADS, rows, HEAD_DIM), F32),
        compiler_params=_params("arbitrary", "arbitrary"),
        name="attn_b_sample",
    )(page_table, q_rows, kv_new, bias, *([pool] * PAGES_PER_STEP))


def _roll_windows_body(*refs, n):
    caches, news, outs, sem = refs[:n], refs[n:2 * n], refs[2 * n:3 * n], refs[3 * n]
    copies = []
    for cache, new, out in zip(caches, news, outs):
        bs, wb, s_new = cache.shape[1], cache.shape[2], new.shape[1]
        for b in range(bs):
            copies.append((cache.at[0, b, pl.ds(s_new, wb - s_new)], out.at[0, b, pl.ds(0, wb - s_new)]))
        copies.append((new, out.at[0, :, pl.ds(wb - s_new, s_new)]))
    dmas = [pltpu.make_async_copy(src, dst, sem.at[idx]) for idx, (src, dst) in enumerate(copies)]
    for dma in dmas:
        dma.start()
    for dma in dmas:
        dma.wait()


def _roll_windows(caches, news):
    n = len(caches)
    n_copies = sum(c.shape[1] + 1 for c in caches)
    any_spec = pl.BlockSpec(memory_space=pl.ANY)
    return pl.pallas_call(
        functools.partial(_roll_windows_body, n=n),
        in_specs=[any_spec] * (2 * n),
        out_specs=[any_spec] * n,
        out_shape=[jax.ShapeDtypeStruct(c.shape, c.dtype) for c in caches],
        scratch_shapes=[pltpu.SemaphoreType.DMA((n_copies,))],
        name="roll_windows",
    )(*caches, *news)


def _mlp(x, gain, w_up, w_down, *, tiles, layer=0, emit_w=False):
    if emit_w:
        h, w_up_b = _norm_matmul(x, gain, w_up, mode="relu2", out_dtype=BF16, tiles=tiles, layer=layer, emit_w=True)
        y, w_down_b = _matmul_res(h, w_down, x, tiles=tiles, layer=layer, emit_w=True)
        return y, w_up_b, w_down_b
    (xn,) = _rms_norm(x, [gain], tr=tiles.tr_norm)
    h = _norm_matmul(xn, None, w_up, mode="relu2", out_dtype=BF16, tiles=tiles)
    return _matmul_res(h, w_down, x, tiles=tiles)


@jax.jit
def _step(x_prompt, x_sample, cache_win_g0, cache_win_g1, cache_win_g2, cache_kv, page_table, bias_table,
          norm_mix, norm_mlp, w_a_in, g_a_q, g_a_k, w_a_out, norm_kv, w_kv, w_b_q, b_sb, w_b_out, w_up, w_down):
    bp, tp, _ = x_prompt.shape
    bs, ts, _ = x_sample.shape
    pt, st = PROMPT_TILES, _sample_tiles(bs * ts)
    xp = x_prompt.reshape(bp * tp, D_MODEL)
    xs = x_sample.reshape(bs * ts, D_MODEL)
    caches = (cache_win_g0[0], cache_win_g1[0], cache_win_g2[0])

    head_gain = jnp.concatenate([jnp.tile(g_a_q[0], N_A_HEADS), jnp.tile(g_a_k[0], N_A_HEADS),
                                 jnp.ones((A_WIDTH,), F32)])
    col_gain = jnp.tile(head_gain, N_GROUPS).reshape(1, -1).astype(F32)
    qkv_s, w_in = _norm_matmul(xs, norm_mix[0], w_a_in, mode="qkv", out_dtype=F32, tiles=st,
                               col_gain=col_gain, emit_w=True)
    (xn_p,) = _rms_norm(xp, [norm_mix[0]], tr=pt.tr_norm)
    qkv_p = _norm_matmul(xn_p, None, w_in, mode="qkv", out_dtype=F32, tiles=pt, col_gain=col_gain)

    bias_m = _bias_by_stride(bias_table)
    qkv_s6 = qkv_s.reshape(bs, ts, N_GROUPS, 3, N_A_HEADS, HEAD_DIM)
    o_s = _attn_a_sample(qkv_s6, caches, bias_m, bs=bs, s_new=ts)
    o_s = o_s.reshape(bs * ts, A_WIDTH).astype(BF16)
    o_p = _attn_a_prompt(qkv_p, _prompt_bias(bias_m), b=bp, t=tp)

    xs, w_ao = _matmul_res(o_s, w_a_out, xs, tiles=st, emit_w=True)
    xp = _matmul_res(o_p, w_ao, xp, tiles=pt)
    xs, w_up0, w_dn0 = _mlp(xs, norm_mlp[0], w_up, w_down, tiles=st, layer=0, emit_w=True)
    xp = _mlp(xp, norm_mlp[0], w_up0, w_dn0, tiles=pt)

    kv_s, w_kvb = _norm_matmul(xs, norm_kv, w_kv, mode="plain", out_dtype=F32, tiles=st, emit_w=True)
    xn_kv, xn_bq = _rms_norm(xp, [norm_kv, norm_mix[1]], tr=pt.tr_norm)
    kv_p = _norm_matmul(xn_kv, None, w_kvb, mode="plain", out_dtype=F32, tiles=pt)

    qb_s, w_bq = _norm_matmul(xs, norm_mix[1], w_b_q, mode="plain", out_dtype=F32, tiles=st, emit_w=True)
    qb_p = _norm_matmul(xn_bq, None, w_bq, mode="plain", out_dtype=BF16, tiles=pt)

    sb = b_sb[0].astype(F32).reshape(N_KV_HEADS, GQ) * LOG2E
    q_rows = qb_s.reshape(bs, ts, N_KV_HEADS, GQ, HEAD_DIM).transpose(0, 2, 3, 1, 4).reshape(
        bs, N_KV_HEADS, GQ * ts, HEAD_DIM)
    kv_s5 = kv_s.reshape(bs, ts, 2, N_KV_HEADS, HEAD_DIM)
    ob_s = _attn_b_sample(q_rows, kv_s5.transpose(0, 2, 3, 1, 4), sb, cache_kv, page_table, s_new=ts)
    ob_s = ob_s.reshape(bs, N_KV_HEADS, GQ, ts, HEAD_DIM).transpose(0, 3, 1, 2, 4).reshape(
        bs * ts, N_B_HEADS * HEAD_DIM).astype(BF16)
    ob_p = _attn_b_prompt(qb_p, kv_p, sb, b=bp, t=tp)

    xs, w_bo = _matmul_res(ob_s, w_b_out, xs, tiles=st, emit_w=True)
    xp = _matmul_res(ob_p, w_bo, xp, tiles=pt)
    xs, w_up1, w_dn1 = _mlp(xs, norm_mlp[1], w_up, w_down, tiles=st, layer=1, emit_w=True)
    xp = _mlp(xp, norm_mlp[1], w_up1, w_dn1, tiles=pt)

    qkv_p6 = qkv_p.reshape(bp, tp, N_GROUPS, 3, N_A_HEADS, HEAD_DIM)
    win_p = [qkv_p6[:, tp - min(window, tp):, g, 1:3][None] for g, (window, _) in enumerate(DILATED_PAIRS)]
    win_s = _roll_windows((cache_win_g0, cache_win_g1, cache_win_g2),
                          [qkv_s6[:, :, g, 1:3] for g in range(N_GROUPS)])
    return (xp.reshape(bp, tp, D_MODEL), xs.reshape(bs, ts, D_MODEL), *win_p, *win_s,
            kv_p.reshape(bp, tp, 2, N_KV_HEADS, HEAD_DIM), kv_s5)


def kernel(x_prompt, x_sample, cache_win_g0, cache_win_g1, cache_win_g2, cache_kv, page_table, bias_table,
           norm_mix, norm_mlp, w_a_in, g_a_q, g_a_k, w_a_out, norm_kv, w_kv, w_b_q, b_sb, w_b_out, w_up, w_down):
    return _step(x_prompt, x_sample, cache_win_g0, cache_win_g1, cache_win_g2, cache_kv, page_table, bias_table,
                 norm_mix, norm_mlp, w_a_in, g_a_q, g_a_k, w_a_out, norm_kv, w_kv, w_b_q, b_sb, w_b_out,
                 w_up, w_down)
```

```python
import functools
import math
from typing import NamedTuple

import jax
import jax.numpy as jnp
from jax import lax
from jax.experimental import pallas as pl
from jax.experimental.pallas import tpu as pltpu

F32 = jnp.float32
BF16 = jnp.bfloat16

D_MODEL = 4096
HEAD_DIM = 128
DILATED_PAIRS = ((128, 1), (512, 4), (2048, 16))
N_GROUPS = len(DILATED_PAIRS)
N_A_HEADS = D_MODEL // (2 * HEAD_DIM)
A_WIDTH = N_A_HEADS * HEAD_DIM
N_B_HEADS = D_MODEL // HEAD_DIM
N_KV_HEADS = 8
GQ = N_B_HEADS // N_KV_HEADS
N_BUCKETS = 32
MAX_DISTANCE = 2048
RMS_EPS = 1e-6
PAGE_SIZE = 128
BAND = 128
ATT_SCALE = HEAD_DIM ** -0.5
MASKED = -1e30

V7X_VMEM_LIMIT_BYTES = 60 * 1024 * 1024
V7X_MXU_COLS = 256
NORM_ROWS = 16
PAGES_PER_STEP = 4
KEY_BLOCKS_PER_SWEEP = 3


class Tiles(NamedTuple):
    tm: int
    tn_full_k: int
    tn_res: int
    tk_res: int
    tr_norm: int


PROMPT_TILES = Tiles(tm=1024, tn_full_k=1024, tn_res=1024, tk_res=4096, tr_norm=512)


def _sample_tiles(rows):
    return Tiles(tm=rows, tn_full_k=512, tn_res=1024, tk_res=2048, tr_norm=rows)


def _params(*semantics):
    return pltpu.CompilerParams(dimension_semantics=semantics,
                                vmem_limit_bytes=V7X_VMEM_LIMIT_BYTES)


def _weight_spec(w, layer, block, index):
    if w.ndim == 2:
        return pl.BlockSpec(block, index)
    return pl.BlockSpec((None,) + block, lambda *g: (layer,) + index(*g))


def _head_rms(t, cg):
    ms = jnp.mean(t * t, axis=-1, keepdims=True)
    return t * lax.rsqrt(ms + RMS_EPS) * cg


def _rms_rows(x_ref, g_refs, o_refs, n_rows):
    per = math.gcd(n_rows // NORM_ROWS, 4)

    def rows_step(c, carry):
        groups = [pl.ds(pl.multiple_of((c * per + u) * NORM_ROWS, NORM_ROWS), NORM_ROWS) for u in range(per)]
        scales = [lax.rsqrt(jnp.mean(jnp.square(x_ref[rows, :]), axis=-1, keepdims=True) + RMS_EPS)
                  for rows in groups]
        for rows, scale in zip(groups, scales):
            y = x_ref[rows, :] * scale
            for g_ref, o_ref in zip(g_refs, o_refs):
                o_ref[rows, :] = (y * g_ref[...]).astype(BF16)
        return carry

    lax.fori_loop(0, n_rows // (NORM_ROWS * per), rows_step, 0)


def _rms_norm_body(x_ref, *refs, n_gains, tr):
    _rms_rows(x_ref, refs[:n_gains], refs[n_gains:], tr)


def _rms_norm(x, gains, *, tr):
    m, k = x.shape
    n = len(gains)
    outs = pl.pallas_call(
        functools.partial(_rms_norm_body, n_gains=n, tr=tr),
        grid=(m // tr,),
        in_specs=[pl.BlockSpec((tr, k), lambda i: (i, 0))] + [pl.BlockSpec((1, k), lambda i: (0, 0))] * n,
        out_specs=[pl.BlockSpec((tr, k), lambda i: (i, 0))] * n,
        out_shape=[jax.ShapeDtypeStruct((m, k), BF16)] * n,
        compiler_params=_params("arbitrary"),
        name=f"rms_norm_{n}",
    )(x, *[g.reshape(1, k) for g in gains])
    return outs


def _norm_matmul_body(x_ref, *refs, mode, tm, tn, emit_w, fused_norm):
    if fused_norm:
        g_ref, w_ref, cg_ref, o_ref, *rest = refs
    else:
        w_ref, cg_ref, o_ref, *rest = refs
    if emit_w:
        wb_ref, *rest = rest
    j = pl.program_id(1)

    if fused_norm:
        (xn_ref,) = rest

        @pl.when(j == 0)
        def _():
            _rms_rows(x_ref, [g_ref], [xn_ref], tm)
    else:
        xn_ref = x_ref

    if emit_w:
        wb_ref[...] = w_ref[...].astype(BF16)
        w_src = wb_ref
    else:
        w_src = w_ref

    if mode == "relu2":
        r = jnp.maximum(jnp.dot(xn_ref[...], w_src[...], preferred_element_type=F32), 0.0)
        o_ref[...] = (r * r).astype(o_ref.dtype)
    elif mode == "plain":
        o_ref[...] = jnp.dot(xn_ref[...], w_src[...], preferred_element_type=F32).astype(o_ref.dtype)
    else:
        kind = (j * tn // A_WIDTH) % 3

        @pl.when(kind == 2)
        def _():
            o_ref[...] = jnp.dot(xn_ref[...], w_src[...], preferred_element_type=F32)

        @pl.when(kind != 2)
        def _():
            for c0 in range(0, tn, V7X_MXU_COLS):
                y = jnp.dot(xn_ref[...], w_src[:, c0:c0 + V7X_MXU_COLS], preferred_element_type=F32)
                for c in range(0, V7X_MXU_COLS, HEAD_DIM):
                    cols = slice(c0 + c, c0 + c + HEAD_DIM)
                    o_ref[:, cols] = _head_rms(y[:, c:c + HEAD_DIM], cg_ref[:, cols])


def _norm_matmul(x, gain, w, *, mode, out_dtype, tiles, layer=0, col_gain=None, emit_w=False):
    m, k = x.shape
    n = w.shape[-1]
    tm, tn = tiles.tm, tiles.tn_full_k
    fused_norm = gain is not None
    assert not emit_w or m == tm
    assert fused_norm or x.dtype == BF16
    if col_gain is None:
        col_gain = jnp.ones((1, n), F32)
    body = functools.partial(_norm_matmul_body, mode=mode, tm=tm, tn=tn, emit_w=emit_w, fused_norm=fused_norm)
    out_specs = pl.BlockSpec((tm, tn), lambda i, j: (i, j))
    out_shape = jax.ShapeDtypeStruct((m, n), out_dtype)
    if emit_w:
        out_specs = [out_specs, pl.BlockSpec((k, tn), lambda i, j: (0, j))]
        out_shape = [out_shape, jax.ShapeDtypeStruct((k, n), BF16)]
    if fused_norm:
        x_specs = [pl.BlockSpec((tm, k), lambda i, j: (i, 0), pipeline_mode=pl.Buffered(1)),
                   pl.BlockSpec((1, k), lambda i, j: (0, 0))]
        x_args = (x, gain.reshape(1, k))
        scratch = [pltpu.VMEM((tm, k), BF16)]
    else:
        x_specs = [pl.BlockSpec((tm, k), lambda i, j: (i, 0))]
        x_args = (x,)
        scratch = []
    return pl.pallas_call(
        body,
        grid=(m // tm, n // tn),
        in_specs=x_specs + [
            _weight_spec(w, layer, (k, tn), lambda i, j: (0, j)),
            pl.BlockSpec((1, tn), lambda i, j: (0, j)),
        ],
        out_specs=out_specs,
        out_shape=out_shape,
        scratch_shapes=scratch,
        compiler_params=_params("arbitrary", "arbitrary"),
        name=f"norm_matmul_{mode}_{tm}",
    )(*x_args, w, col_gain)


def _matmul_res_body(a_ref, w_ref, r_ref, o_ref, *rest, emit_w):
    kk = pl.program_id(2)
    if emit_w:
        (wb_ref,) = rest
        wb_ref[...] = w_ref[...].astype(BF16)
        w_src = wb_ref
    else:
        w_src = w_ref
    d = jnp.dot(a_ref[...], w_src[...], preferred_element_type=F32)

    @pl.when(kk == 0)
    def _():
        o_ref[...] = r_ref[...] + d

    @pl.when(kk != 0)
    def _():
        o_ref[...] += d


def _matmul_res(a, w, res, *, tiles, layer=0, emit_w=False):
    m, k = a.shape
    n = w.shape[-1]
    tm, tn, tk = tiles.tm, tiles.tn_res, min(tiles.tk_res, k)
    assert not emit_w or m == tm
    out_specs = pl.BlockSpec((tm, tn), lambda i, j, kk: (i, j))
    out_shape = jax.ShapeDtypeStruct((m, n), F32)
    if emit_w:
        out_specs = [out_specs, pl.BlockSpec((tk, tn), lambda i, j, kk: (kk, j))]
        out_shape = [out_shape, jax.ShapeDtypeStruct((k, n), BF16)]
    return pl.pallas_call(
        functools.partial(_matmul_res_body, emit_w=emit_w),
        grid=(m // tm, n // tn, k // tk),
        in_specs=[
            pl.BlockSpec((tm, tk), lambda i, j, kk: (i, kk)),
            _weight_spec(w, layer, (tk, tn), lambda i, j, kk: (kk, j)),
            pl.BlockSpec((tm, tn), lambda i, j, kk: (i, j)),
        ],
        out_specs=out_specs,
        out_shape=out_shape,
        compiler_params=_params("arbitrary", "arbitrary", "arbitrary"),
        name=f"matmul_res_{tm}_{k}",
    )(a, w, res)


def _rel_bucket(dist):
    max_exact = N_BUCKETS // 2
    d32 = jnp.maximum(dist, 1).astype(F32)
    large = max_exact + (jnp.log(d32 / max_exact) / math.log(MAX_DISTANCE / max_exact)
                         * (N_BUCKETS - max_exact)).astype(jnp.int32)
    large = jnp.minimum(large, N_BUCKETS - 1)
    return jnp.where(dist < max_exact, dist, large)


def _bias_by_stride(bias_table):
    m = jnp.arange(BAND + 1)
    return jnp.stack([bias_table[:, g][_rel_bucket(m * d)] for g, (_, d) in enumerate(DILATED_PAIRS)])


def _prompt_bias(bias_m):
    period = 2 * BAND + 1
    u = jnp.concatenate([bias_m[:, ::-1, :], jnp.full((N_GROUPS, period - BAND - 1, N_A_HEADS), MASKED, F32)],
                        axis=1).transpose(0, 2, 1)
    flat = jnp.tile(u, (1, 1, BAND))[:, :, :BAND * 2 * BAND]
    return flat.reshape(N_GROUPS, N_A_HEADS, BAND, 2 * BAND)


def _attn_a_prompt_body(q0, k0, v0, q1, k1, v1, q2, k2, v2, bias_ref, o_ref, og_ref, lse_ref, *, t):
    qkv = ((q0, k0, v0), (q1, k1, v1), (q2, k2, v2))
    ones = jnp.ones((BAND, HEAD_DIM), BF16)
    dn = (((1,), (1,)), ((), ()))

    def rows_of(start, d):
        if d == 1:
            return pl.ds(start, BAND)
        return pl.ds(start, BAND, stride=d)

    def blocks(g, specs):
        q_ref, k_ref, v_ref = qkv[g]
        with_prev = specs[0][1] is not None

        def logits(q, rows, bias):
            return lax.dot_general(q, k_ref[rows, :].astype(BF16), dn, preferred_element_type=F32) * ATT_SCALE + bias

        def weighted(p, rows):
            vx = jnp.concatenate([v_ref[rows, :].astype(BF16), ones], axis=1)
            return jnp.dot(p.astype(BF16), vx, preferred_element_type=F32)

        qs = [q_ref[cur, :].astype(BF16) for cur, _ in specs]
        s_cs = [logits(q, cur, bias_ref[g, :, BAND:]) for q, (cur, _) in zip(qs, specs)]
        mxs = [jnp.max(s, axis=-1, keepdims=True) for s in s_cs]
        if with_prev:
            s_ps = [logits(q, prev, bias_ref[g, :, :BAND]) for q, (_, prev) in zip(qs, specs)]
            mxs = [jnp.maximum(mx, jnp.max(s, axis=-1, keepdims=True)) for mx, s in zip(mxs, s_ps)]
        accs = [weighted(jnp.exp(s - mx), cur) for s, mx, (cur, _) in zip(s_cs, mxs, specs)]
        if with_prev:
            accs = [acc + weighted(jnp.exp(s - mx), prev) for acc, s, mx, (_, prev) in zip(accs, s_ps, mxs, specs)]
        for acc, mx, (cur, _) in zip(accs, mxs, specs):
            den = acc[:, HEAD_DIM:]
            og_ref[g, cur, :] = acc[:, :HEAD_DIM] / den
            lse_ref[g, cur, :] = mx + jnp.log(den)

    for g, (_, d) in enumerate(DILATED_PAIRS):
        span = BAND * d
        n_blk = t // span
        if n_blk == 1:
            per = 8

            def residue_step(it, carry, g=g, d=d, per=per):
                blocks(g, [(rows_of(it * per + u, d), None) for u in range(per)])
                return carry
            lax.fori_loop(0, d // per, residue_step, 0)
        elif d == 1:
            per = 5
            assert (n_blk - 1) % per == 0
            blocks(g, [(rows_of(0, d), None)])

            def blk_step(it, carry, g=g, d=d, per=per, span=span):
                starts = [pl.multiple_of((1 + it * per + u) * span, BAND) for u in range(per)]
                blocks(g, [(rows_of(st, d), rows_of(st - span, d)) for st in starts])
                return carry
            lax.fori_loop(0, (n_blk - 1) // per, blk_step, 0)
        else:
            blocks(g, [(rows_of(r, d), None) for r in range(d)])

            def blk_step(n, carry, g=g, d=d, span=span):
                blocks(g, [(rows_of(n * span + r, d), rows_of(n * span + r - span, d)) for r in range(d)])
                return carry
            lax.fori_loop(1, n_blk, blk_step, 0)

    def merge_step(c, carry):
        rows = pl.ds(pl.multiple_of(c * BAND, BAND), BAND)
        l0, l1, l2 = lse_ref[0, rows, :], lse_ref[1, rows, :], lse_ref[2, rows, :]
        mx = jnp.maximum(jnp.maximum(l0, l1), l2)
        e0, e1, e2 = jnp.exp(l0 - mx), jnp.exp(l1 - mx), jnp.exp(l2 - mx)
        tot = e0 + e1 + e2
        o = (e0 / tot) * og_ref[0, rows, :] + (e1 / tot) * og_ref[1, rows, :] + (e2 / tot) * og_ref[2, rows, :]
        o_ref[rows, :] = o.astype(o_ref.dtype)
        return carry

    lax.fori_loop(0, t // BAND, merge_step, 0)


def _attn_a_prompt(qkv, bias, *, b, t):
    heads = N_A_HEADS

    def col_spec(g, c):
        base = (g * 3 + c) * heads
        return pl.BlockSpec((t, HEAD_DIM), lambda bi, h: (bi, base + h))

    in_specs = [col_spec(g, c) for g in range(N_GROUPS) for c in range(3)]
    in_specs.append(pl.BlockSpec((N_GROUPS, None, BAND, 2 * BAND), lambda bi, h: (0, h, 0, 0)))
    return pl.pallas_call(
        functools.partial(_attn_a_prompt_body, t=t),
        grid=(b, heads),
        in_specs=in_specs,
        out_specs=pl.BlockSpec((t, HEAD_DIM), lambda bi, h: (bi, h)),
        out_shape=jax.ShapeDtypeStruct((b * t, A_WIDTH), BF16),
        scratch_shapes=[pltpu.VMEM((N_GROUPS, t, HEAD_DIM), F32),
                        pltpu.VMEM((N_GROUPS, t, HEAD_DIM), F32)],
        compiler_params=_params("arbitrary", "arbitrary"),
        name="attn_a_prompt",
    )(*([qkv] * 9), bias)


def _attn_a_sample_body(qkv_ref, c0_ref, c1_ref, c2_ref, bb0_ref, bb12_ref, bn0_ref, bn12_ref, o_ref):
    s = pl.program_id(1)
    outs, lses = [], []
    for g in range(N_GROUPS):
        q = qkv_ref[s, g, 0]
        if g == 0:
            kb, vb, bias_b = c0_ref[:, 0], c0_ref[:, 1], bb0_ref[...]
            kn, vn, bias_n = qkv_ref[:, 0, 1], qkv_ref[:, 0, 2], bn0_ref[...]
        else:
            c_ref = c1_ref if g == 1 else c2_ref
            kb, vb, bias_b = c_ref[:, 0], c_ref[:, 1], bb12_ref[g - 1]
            kn, vn, bias_n = qkv_ref[pl.ds(s, 1), g, 1], qkv_ref[pl.ds(s, 1), g, 2], bn12_ref[g - 1][None]
        lb = jnp.sum(q[None] * kb, axis=-1, keepdims=True) * ATT_SCALE + bias_b
        ln = jnp.sum(q[None] * kn, axis=-1, keepdims=True) * ATT_SCALE + bias_n
        mx = jnp.maximum(jnp.max(lb, axis=0), jnp.max(ln, axis=0))
        pb = jnp.exp(lb - mx[None])
        pn = jnp.exp(ln - mx[None])
        den = jnp.sum(pb, axis=0) + jnp.sum(pn, axis=0)
        acc = jnp.sum(pb * vb, axis=0) + jnp.sum(pn * vn, axis=0)
        outs.append(acc / den)
        lses.append(mx + jnp.log(den))
    mx = jnp.maximum(jnp.maximum(lses[0], lses[1]), lses[2])
    es = [jnp.exp(l - mx) for l in lses]
    tot = es[0] + es[1] + es[2]
    o_ref[...] = (es[0] / tot) * outs[0] + (es[1] / tot) * outs[1] + (es[2] / tot) * outs[2]


def _attn_a_sample(qkv_s, caches, bias_m, *, bs, s_new):
    h = N_A_HEADS
    lane = (h, HEAD_DIM)

    def dense(x):
        return jnp.broadcast_to(x[..., None], x.shape + (HEAD_DIM,)).astype(F32)

    sq = jnp.arange(s_new)
    r = jnp.arange(BAND)
    m_b = BAND + sq[:, None] - r[None, :]
    bb0 = dense(jnp.where((m_b <= BAND)[..., None], bias_m[0][jnp.clip(m_b, 0, BAND)], MASKED))
    m_n = sq[:, None] - sq[None, :]
    bn0 = dense(jnp.where((m_n >= 0)[..., None], bias_m[0][jnp.clip(m_n, 0, BAND)], MASKED))
    bb12 = dense(jnp.stack([bias_m[g][BAND - r] for g in (1, 2)]))
    bn12 = dense(jnp.stack([bias_m[g][0] for g in (1, 2)]))

    c0 = caches[0]
    c1 = caches[1].reshape(bs, BAND, DILATED_PAIRS[1][1], 2, h, HEAD_DIM)
    c2 = caches[2].reshape(bs, BAND, DILATED_PAIRS[2][1], 2, h, HEAD_DIM)
    return pl.pallas_call(
        _attn_a_sample_body,
        grid=(bs, s_new),
        in_specs=[
            pl.BlockSpec((None, s_new, N_GROUPS, 3) + lane, lambda b, s: (b, 0, 0, 0, 0, 0)),
            pl.BlockSpec((None, BAND, 2) + lane, lambda b, s: (b, 0, 0, 0, 0)),
            pl.BlockSpec((None, BAND, None, 2) + lane, lambda b, s: (b, 0, s, 0, 0, 0)),
            pl.BlockSpec((None, BAND, None, 2) + lane, lambda b, s: (b, 0, s, 0, 0, 0)),
            pl.BlockSpec((None, BAND) + lane, lambda b, s: (s, 0, 0, 0)),
            pl.BlockSpec((2, BAND) + lane, lambda b, s: (0, 0, 0, 0)),
            pl.BlockSpec((None, s_new) + lane, lambda b, s: (s, 0, 0, 0)),
            pl.BlockSpec((2,) + lane, lambda b, s: (0, 0, 0)),
        ],
        out_specs=pl.BlockSpec((None, None) + lane, lambda b, s: (b, s, 0, 0)),
        out_shape=jax.ShapeDtypeStruct((bs, s_new) + lane, F32),
        compiler_params=_params("arbitrary", "arbitrary"),
        name="attn_a_sample",
    )(qkv_s, c0, c1, c2, bb0, bb12, bn0, bn12)


LOG2E = 1.0 / math.log(2.0)
SB_LOGIT_SCALE = ATT_SCALE * LOG2E


def _softplus2(z2):
    return jnp.maximum(z2, 0.0) + jnp.log2(1.0 + jnp.exp2(-jnp.abs(z2)))


def _suffix_matrix():
    n = 2 * PAGE_SIZE
    j = lax.broadcasted_iota(jnp.int32, (n, n), 0) % PAGE_SIZE
    c = lax.broadcasted_iota(jnp.int32, (n, n), 1)
    return jnp.where((c >= PAGE_SIZE) | (j > c), 1.0, 0.0).astype(BF16)


def _hi_lo(l):
    hi = l.astype(BF16)
    lo = (l - hi.astype(F32)).astype(BF16)
    return jnp.concatenate([hi, lo], axis=1)


def _attn_b_prompt_body(q_ref, k_ref, v_ref, bias_ref, o_ref, kb_ref, vb_ref, *, t, tq):
    i = pl.program_id(2)

    @pl.when(i == 0)
    def _():
        def cast_step(c, carry):
            rows = pl.ds(pl.multiple_of(c * 256, 256), 256)
            kb_ref[rows, :] = (k_ref[rows, :] * SB_LOGIT_SCALE).astype(BF16)
            vb_ref[rows, :] = v_ref[rows, :].astype(BF16)
            return carry
        lax.fori_loop(0, t // 256, cast_step, 0)

    suffix = _suffix_matrix()
    dn = (((1,), (1,)), ((), ()))
    heads = range(GQ)

    def block_rows(jlast, nblk):
        return pl.ds(pl.multiple_of((jlast - (nblk - 1)) * PAGE_SIZE, PAGE_SIZE), nblk * PAGE_SIZE)

    def logits(rows, nblk):
        k = kb_ref[rows, :]
        return [lax.dot_general(q_ref[:, g * HEAD_DIM:(g + 1) * HEAD_DIM], k, dn, preferred_element_type=F32)
                + bias_ref[g, :, :nblk * PAGE_SIZE] for g in heads]

    def suffix_sums(zs, nblk, visible):
        sps = [_softplus2(z) for z in zs]
        ls = [-sp if visible is None else jnp.where(visible, -sp, 0.0) for sp in sps]
        ss = [jnp.dot(jnp.concatenate([_hi_lo(l[:, u * PAGE_SIZE:(u + 1) * PAGE_SIZE])
                                       for u in reversed(range(nblk))], axis=0),
                      suffix, preferred_element_type=F32) for l in ls]
        return sps, ss

    def weights(zs, sps, ss, runs, nblk, visible):
        avals, new_runs = [], []
        for g, (z, sp, s) in enumerate(zip(zs, sps, ss)):
            run = None if runs is None else runs[g]
            later = [None] * nblk
            for idx in range(nblk):
                su = s[idx * tq:(idx + 1) * tq]
                later[nblk - 1 - idx] = su[:, :PAGE_SIZE] if run is None else su[:, :PAGE_SIZE] + run
                run = su[:, PAGE_SIZE:] if run is None else run + su[:, PAGE_SIZE:]
            a = jnp.exp2(z - sp + (later[0] if nblk == 1 else jnp.concatenate(later, axis=1)))
            if visible is not None:
                a = jnp.where(visible, a, 0.0)
            avals.append(a.astype(BF16))
            new_runs.append(run)
        return avals, tuple(new_runs)

    def weighted_values(avals, rows):
        v = vb_ref[rows, :]
        return [jnp.dot(a, v, preferred_element_type=F32) for a in avals]

    def sweep(jlast, nblk, state):
        rows = block_rows(jlast, nblk)
        visible = None
        if state is None:
            qpos = lax.broadcasted_iota(jnp.int32, (tq, nblk * PAGE_SIZE), 0)
            kpos = lax.broadcasted_iota(jnp.int32, (tq, nblk * PAGE_SIZE), 1) - (nblk - 1) * PAGE_SIZE
            visible = kpos < qpos
        zs = logits(rows, nblk)
        sps, ss = suffix_sums(zs, nblk, visible)
        avals, runs = weights(zs, sps, ss, None if state is None else state[0], nblk, visible)
        pvs = weighted_values(avals, rows)
        accs = pvs if state is None else [acc + pv for acc, pv in zip(state[1], pvs)]
        return runs, tuple(accs)

    total = i + 1
    n_groups = lax.shift_right_logical(total * 11, 5)
    rem = total - n_groups * KEY_BLOCKS_PER_SWEEP
    state = lax.cond(rem == 1, lambda: sweep(i, 1, None),
                     lambda: lax.cond(rem == 2, lambda: sweep(i, 2, None),
                                      lambda: sweep(i, KEY_BLOCKS_PER_SWEEP, None)))
    first_size = jnp.where(rem == 0, KEY_BLOCKS_PER_SWEEP, rem)
    n_rest = n_groups - jnp.where(rem == 0, 1, 0)
    first = i - first_size
    _, accs = lax.fori_loop(0, n_rest,
                            lambda p, st: sweep(first - KEY_BLOCKS_PER_SWEEP * p, KEY_BLOCKS_PER_SWEEP, st), state)
    for g in heads:
        o_ref[:, g * HEAD_DIM:(g + 1) * HEAD_DIM] = accs[g].astype(o_ref.dtype)


def _attn_b_prompt(q, kv, sb, *, b, t):
    tq = PAGE_SIZE
    nq = t // tq
    assert KEY_BLOCKS_PER_SWEEP == 3 and nq <= 16
    bias = jnp.broadcast_to(sb[:, :, None, None], (N_KV_HEADS, GQ, 1, KEY_BLOCKS_PER_SWEEP * PAGE_SIZE))
    return pl.pallas_call(
        functools.partial(_attn_b_prompt_body, t=t, tq=tq),
        grid=(b, N_KV_HEADS, nq),
        in_specs=[
            pl.BlockSpec((tq, GQ * HEAD_DIM), lambda bi, kh, i: (bi * nq + i, kh)),
            pl.BlockSpec((t, HEAD_DIM), lambda bi, kh, i: (bi, kh)),
            pl.BlockSpec((t, HEAD_DIM), lambda bi, kh, i: (bi, N_KV_HEADS + kh)),
            pl.BlockSpec((None, GQ, 1, KEY_BLOCKS_PER_SWEEP * PAGE_SIZE), lambda bi, kh, i: (kh, 0, 0, 0)),
        ],
        out_specs=pl.BlockSpec((tq, GQ * HEAD_DIM), lambda bi, kh, i: (bi * nq + i, kh)),
        out_shape=jax.ShapeDtypeStruct((b * t, N_B_HEADS * HEAD_DIM), BF16),
        scratch_shapes=[pltpu.VMEM((t, HEAD_DIM), BF16), pltpu.VMEM((t, HEAD_DIM), BF16)],
        compiler_params=_params("arbitrary", "arbitrary", "arbitrary"),
        name="attn_b_prompt",
    )(q, kv, kv, bias)


def _attn_b_sample_body(pt_ref, q_ref, kvn_ref, bias_ref, *rest, s_new, n_steps):
    page_refs = rest[:PAGES_PER_STEP]
    o_ref, run_ref, acc_ref = rest[PAGES_PER_STEP:]
    p = pl.program_id(1)
    rows = GQ * s_new
    row_tok = lax.broadcasted_iota(jnp.int32, (rows, HEAD_DIM), 0) % s_new

    @pl.when(p == 0)
    def _():
        for kh in range(N_KV_HEADS):
            q = q_ref[kh]
            run = jnp.zeros((rows, HEAD_DIM), F32)
            acc = jnp.zeros((rows, HEAD_DIM), F32)
            for j in reversed(range(s_new)):
                kj = kvn_ref[0, kh, pl.ds(j, 1), :]
                vj = kvn_ref[1, kh, pl.ds(j, 1), :]
                z = jnp.sum(q * kj, axis=-1, keepdims=True) * SB_LOGIT_SCALE + bias_ref[kh]
                valid = row_tok > j
                sp = _softplus2(z)
                a = jnp.where(valid, jnp.exp2(z - sp + run), 0.0)
                acc = acc + a * vj
                run = run + jnp.where(valid, -sp, 0.0)
            run_ref[kh] = run
            acc_ref[kh] = acc

    suffix = _suffix_matrix()
    stride = 2 * N_KV_HEADS
    dn = (((1,), (1,)), ((), ()))
    def head_rows(first):
        return jnp.concatenate([r[pl.ds(first, PAGE_SIZE, stride=stride), :] for r in page_refs], axis=0).astype(BF16)

    def later_and_run(s, run):
        later = []
        for u in range(PAGES_PER_STEP):
            su = s[u * rows:(u + 1) * rows]
            later.append(su[:, :PAGE_SIZE] + run)
            run = run + su[:, PAGE_SIZE:]
        return jnp.concatenate(later, axis=1), run

    heads = range(N_KV_HEADS)
    zs = [lax.dot_general(q_ref[kh].astype(BF16), head_rows(kh), dn, preferred_element_type=F32) * SB_LOGIT_SCALE
          + jnp.tile(bias_ref[kh], (1, PAGES_PER_STEP)) for kh in heads]
    sps = [_softplus2(z) for z in zs]
    ss = [jnp.dot(jnp.concatenate([_hi_lo(-sp[:, u * PAGE_SIZE:(u + 1) * PAGE_SIZE])
                                   for u in range(PAGES_PER_STEP)], axis=0),
                  suffix, preferred_element_type=F32) for sp in sps]
    lrs = [later_and_run(s, run_ref[kh]) for kh, s in zip(heads, ss)]
    avals = [jnp.exp2(z - sp + later) for z, sp, (later, _) in zip(zs, sps, lrs)]
    pvs = [jnp.dot(a.astype(BF16), head_rows(N_KV_HEADS + kh), preferred_element_type=F32)
           for kh, a in zip(heads, avals)]
    for kh in heads:
        acc_ref[kh] += pvs[kh]
        run_ref[kh] = lrs[kh][1]

    @pl.when(p == n_steps - 1)
    def _():
        o_ref[...] = acc_ref[...]


def _attn_b_sample(q_rows, kv_new, sb, cache_kv, page_table, *, s_new):
    bs, n_pages = page_table.shape
    assert n_pages % PAGES_PER_STEP == 0
    n_steps = n_pages // PAGES_PER_STEP
    rows = GQ * s_new
    bias = jnp.broadcast_to(sb[:, :, None, None], (N_KV_HEADS, GQ, s_new, HEAD_DIM)).reshape(
        N_KV_HEADS, rows, HEAD_DIM)
    pool = cache_kv.reshape(cache_kv.shape[0], PAGE_SIZE * 2 * N_KV_HEADS, HEAD_DIM)

    def page_spec(u):
        return pl.BlockSpec((None, PAGE_SIZE * 2 * N_KV_HEADS, HEAD_DIM),
                            lambda b, p, pt: (pt[b, n_pages - 1 - (p * PAGES_PER_STEP + u)], 0, 0))

    grid_spec = pltpu.PrefetchScalarGridSpec(
        num_scalar_prefetch=1,
        grid=(bs, n_steps),
        in_specs=[
            pl.BlockSpec((None, N_KV_HEADS, rows, HEAD_DIM), lambda b, p, pt: (b, 0, 0, 0)),
            pl.BlockSpec((None, 2, N_KV_HEADS, s_new, HEAD_DIM), lambda b, p, pt: (b, 0, 0, 0, 0)),
            pl.BlockSpec((N_KV_HEADS, rows, HEAD_DIM), lambda b, p, pt: (0, 0, 0)),
        ] + [page_spec(u) for u in range(PAGES_PER_STEP)],
        out_specs=pl.BlockSpec((None, N_KV_HEADS, rows, HEAD_DIM), lambda b, p, pt: (b, 0, 0, 0)),
        scratch_shapes=[pltpu.VMEM((N_KV_HEADS, rows, HEAD_DIM), F32),
                        pltpu.VMEM((N_KV_HEADS, rows, HEAD_DIM), F32)],
    )
    return pl.pallas_call(
        functools.partial(_attn_b_sample_body, s_new=s_new, n_steps=n_steps),
        grid_spec=grid_spec,
        out_shape=jax.ShapeDtypeStruct((bs, N_KV_HEADS, rows, HEAD_DIM), F32),
        compiler_params=_params("arbitrary", "arbitrary"),
        name="attn_b_sample",
    )(page_table, q_rows, kv_new, bias, *([pool] * PAGES_PER_STEP))


def _mlp(x, gain, w_up, w_down, *, tiles, layer=0, emit_w=False):
    if emit_w:
        h, w_up_b = _norm_matmul(x, gain, w_up, mode="relu2", out_dtype=BF16, tiles=tiles, layer=layer, emit_w=True)
        y, w_down_b = _matmul_res(h, w_down, x, tiles=tiles, layer=layer, emit_w=True)
        return y, w_up_b, w_down_b
    (xn,) = _rms_norm(x, [gain], tr=tiles.tr_norm)
    h = _norm_matmul(xn, None, w_up, mode="relu2", out_dtype=BF16, tiles=tiles)
    return _matmul_res(h, w_down, x, tiles=tiles)


@jax.jit
def _step(x_prompt, x_sample, cache_win_g0, cache_win_g1, cache_win_g2, cache_kv, page_table, bias_table,
          norm_mix, norm_mlp, w_a_in, g_a_q, g_a_k, w_a_out, norm_kv, w_kv, w_b_q, b_sb, w_b_out, w_up, w_down):
    bp, tp, _ = x_prompt.shape
    bs, ts, _ = x_sample.shape
    pt, st = PROMPT_TILES, _sample_tiles(bs * ts)
    xp = x_prompt.reshape(bp * tp, D_MODEL)
    xs = x_sample.reshape(bs * ts, D_MODEL)
    caches = (cache_win_g0[0], cache_win_g1[0], cache_win_g2[0])

    head_gain = jnp.concatenate([jnp.tile(g_a_q[0], N_A_HEADS), jnp.tile(g_a_k[0], N_A_HEADS),
                                 jnp.ones((A_WIDTH,), F32)])
    col_gain = jnp.tile(head_gain, N_GROUPS).reshape(1, -1).astype(F32)
    qkv_s, w_in = _norm_matmul(xs, norm_mix[0], w_a_in, mode="qkv", out_dtype=F32, tiles=st,
                               col_gain=col_gain, emit_w=True)
    (xn_p,) = _rms_norm(xp, [norm_mix[0]], tr=pt.tr_norm)
    qkv_p = _norm_matmul(xn_p, None, w_in, mode="qkv", out_dtype=F32, tiles=pt, col_gain=col_gain)

    bias_m = _bias_by_stride(bias_table)
    qkv_s6 = qkv_s.reshape(bs, ts, N_GROUPS, 3, N_A_HEADS, HEAD_DIM)
    o_s = _attn_a_sample(qkv_s6, caches, bias_m, bs=bs, s_new=ts)
    o_s = o_s.reshape(bs * ts, A_WIDTH).astype(BF16)
    o_p = _attn_a_prompt(qkv_p, _prompt_bias(bias_m), b=bp, t=tp)

    xs, w_ao = _matmul_res(o_s, w_a_out, xs, tiles=st, emit_w=True)
    xp = _matmul_res(o_p, w_ao, xp, tiles=pt)
    xs, w_up0, w_dn0 = _mlp(xs, norm_mlp[0], w_up, w_down, tiles=st, layer=0, emit_w=True)
    xp = _mlp(xp, norm_mlp[0], w_up0, w_dn0, tiles=pt)

    kv_s, w_kvb = _norm_matmul(xs, norm_kv, w_kv, mode="plain", out_dtype=F32, tiles=st, emit_w=True)
    xn_kv, xn_bq = _rms_norm(xp, [norm_kv, norm_mix[1]], tr=pt.tr_norm)
    kv_p = _norm_matmul(xn_kv, None, w_kvb, mode="plain", out_dtype=F32, tiles=pt)

    qb_s, w_bq = _norm_matmul(xs, norm_mix[1], w_b_q, mode="plain", out_dtype=F32, tiles=st, emit_w=True)
    qb_p = _norm_matmul(xn_bq, None, w_bq, mode="plain", out_dtype=BF16, tiles=pt)

    sb = b_sb[0].astype(F32).reshape(N_KV_HEADS, GQ) * LOG2E
    q_rows = qb_s.reshape(bs, ts, N_KV_HEADS, GQ, HEAD_DIM).transpose(0, 2, 3, 1, 4).reshape(
        bs, N_KV_HEADS, GQ * ts, HEAD_DIM)
    kv_s5 = kv_s.reshape(bs, ts, 2, N_KV_HEADS, HEAD_DIM)
    ob_s = _attn_b_sample(q_rows, kv_s5.transpose(0, 2, 3, 1, 4), sb, cache_kv, page_table, s_new=ts)
    ob_s = ob_s.reshape(bs, N_KV_HEADS, GQ, ts, HEAD_DIM).transpose(0, 3, 1, 2, 4).reshape(
        bs * ts, N_B_HEADS * HEAD_DIM).astype(BF16)
    ob_p = _attn_b_prompt(qb_p, kv_p, sb, b=bp, t=tp)

    xs, w_bo = _matmul_res(ob_s, w_b_out, xs, tiles=st, emit_w=True)
    xp = _matmul_res(ob_p, w_bo, xp, tiles=pt)
    xs, w_up1, w_dn1 = _mlp(xs, norm_mlp[1], w_up, w_down, tiles=st, layer=1, emit_w=True)
    xp = _mlp(xp, norm_mlp[1], w_up1, w_dn1, tiles=pt)

    qkv_p6 = qkv_p.reshape(bp, tp, N_GROUPS, 3, N_A_HEADS, HEAD_DIM)
    win_p = [qkv_p6[:, tp - min(window, tp):, g, 1:3][None] for g, (window, _) in enumerate(DILATED_PAIRS)]
    win_s = [jnp.concatenate([caches[g][:, ts:], qkv_s6[:, :, g, 1:3]], axis=1)[None] for g in range(N_GROUPS)]
    return (xp.reshape(bp, tp, D_MODEL), xs.reshape(bs, ts, D_MODEL), *win_p, *win_s,
            kv_p.reshape(bp, tp, 2, N_KV_HEADS, HEAD_DIM), kv_s5)


def kernel(x_prompt, x_sample, cache_win_g0, cache_win_g1, cache_win_g2, cache_kv, page_table, bias_table,
           norm_mix, norm_mlp, w_a_in, g_a_q, g_a_k, w_a_out, norm_kv, w_kv, w_b_q, b_sb, w_b_out, w_up, w_down):
    return _step(x_prompt, x_sample, cache_win_g0, cache_win_g1, cache_win_g2, cache_kv, page_table, bias_table,
                 norm_mix, norm_mlp, w_a_in, g_a_q, g_a_k, w_a_out, norm_kv, w_kv, w_b_q, b_sb, w_b_out,
                 w_up, w_down)
```

```python
import functools
import math
from typing import NamedTuple

import jax
import jax.numpy as jnp
from jax import lax
from jax.experimental import pallas as pl
from jax.experimental.pallas import tpu as pltpu

F32 = jnp.float32
BF16 = jnp.bfloat16

D_MODEL = 4096
HEAD_DIM = 128
DILATED_PAIRS = ((128, 1), (512, 4), (2048, 16))
N_GROUPS = len(DILATED_PAIRS)
N_A_HEADS = D_MODEL // (2 * HEAD_DIM)
A_WIDTH = N_A_HEADS * HEAD_DIM
N_B_HEADS = D_MODEL // HEAD_DIM
N_KV_HEADS = 8
GQ = N_B_HEADS // N_KV_HEADS
N_BUCKETS = 32
MAX_DISTANCE = 2048
RMS_EPS = 1e-6
PAGE_SIZE = 128
BAND = 128
ATT_SCALE = HEAD_DIM ** -0.5
MASKED = -1e30

V7X_VMEM_LIMIT_BYTES = 60 * 1024 * 1024
V7X_MXU_COLS = 256
NORM_ROWS = 16
PAGES_PER_STEP = 4
KEY_BLOCKS_PER_SWEEP = 3


class Tiles(NamedTuple):
    tm: int
    tn_full_k: int
    tn_res: int
    tk_res: int
    tr_norm: int


PROMPT_TILES = Tiles(tm=1024, tn_full_k=1024, tn_res=1024, tk_res=4096, tr_norm=512)


def _sample_tiles(rows):
    return Tiles(tm=rows, tn_full_k=512, tn_res=1024, tk_res=2048, tr_norm=rows)


def _params(*semantics):
    return pltpu.CompilerParams(dimension_semantics=semantics,
                                vmem_limit_bytes=V7X_VMEM_LIMIT_BYTES)


def _weight_spec(w, layer, block, index):
    if w.ndim == 2:
        return pl.BlockSpec(block, index)
    return pl.BlockSpec((None,) + block, lambda *g: (layer,) + index(*g))


def _head_rms(t, cg):
    ms = jnp.mean(t * t, axis=-1, keepdims=True)
    return t * lax.rsqrt(ms + RMS_EPS) * cg


def _rms_rows(x_ref, g_refs, o_refs, n_rows):
    per = math.gcd(n_rows // NORM_ROWS, 4)

    def rows_step(c, carry):
        groups = [pl.ds(pl.multiple_of((c * per + u) * NORM_ROWS, NORM_ROWS), NORM_ROWS) for u in range(per)]
        scales = [lax.rsqrt(jnp.mean(jnp.square(x_ref[rows, :]), axis=-1, keepdims=True) + RMS_EPS)
                  for rows in groups]
        for rows, scale in zip(groups, scales):
            y = x_ref[rows, :] * scale
            for g_ref, o_ref in zip(g_refs, o_refs):
                o_ref[rows, :] = (y * g_ref[...]).astype(BF16)
        return carry

    lax.fori_loop(0, n_rows // (NORM_ROWS * per), rows_step, 0)


def _rms_norm_body(x_ref, *refs, n_gains, tr):
    _rms_rows(x_ref, refs[:n_gains], refs[n_gains:], tr)


def _rms_norm(x, gains, *, tr):
    m, k = x.shape
    n = len(gains)
    outs = pl.pallas_call(
        functools.partial(_rms_norm_body, n_gains=n, tr=tr),
        grid=(m // tr,),
        in_specs=[pl.BlockSpec((tr, k), lambda i: (i, 0))] + [pl.BlockSpec((1, k), lambda i: (0, 0))] * n,
        out_specs=[pl.BlockSpec((tr, k), lambda i: (i, 0))] * n,
        out_shape=[jax.ShapeDtypeStruct((m, k), BF16)] * n,
        compiler_params=_params("arbitrary"),
        name=f"rms_norm_{n}",
    )(x, *[g.reshape(1, k) for g in gains])
    return outs


def _norm_matmul_body(x_ref, *refs, mode, tm, tn, emit_w, fused_norm):
    if fused_norm:
        g_ref, w_ref, cg_ref, o_ref, *rest = refs
    else:
        w_ref, cg_ref, o_ref, *rest = refs
    if emit_w:
        wb_ref, *rest = rest
    j = pl.program_id(1)

    if fused_norm:
        (xn_ref,) = rest

        @pl.when(j == 0)
        def _():
            _rms_rows(x_ref, [g_ref], [xn_ref], tm)
    else:
        xn_ref = x_ref

    if emit_w:
        wb_ref[...] = w_ref[...].astype(BF16)
        w_src = wb_ref
    else:
        w_src = w_ref

    if mode == "relu2":
        r = jnp.maximum(jnp.dot(xn_ref[...], w_src[...], preferred_element_type=F32), 0.0)
        o_ref[...] = (r * r).astype(o_ref.dtype)
    elif mode == "plain":
        o_ref[...] = jnp.dot(xn_ref[...], w_src[...], preferred_element_type=F32).astype(o_ref.dtype)
    else:
        kind = (j * tn // A_WIDTH) % 3

        @pl.when(kind == 2)
        def _():
            o_ref[...] = jnp.dot(xn_ref[...], w_src[...], preferred_element_type=F32)

        @pl.when(kind != 2)
        def _():
            for c0 in range(0, tn, V7X_MXU_COLS):
                y = jnp.dot(xn_ref[...], w_src[:, c0:c0 + V7X_MXU_COLS], preferred_element_type=F32)
                for c in range(0, V7X_MXU_COLS, HEAD_DIM):
                    cols = slice(c0 + c, c0 + c + HEAD_DIM)
                    o_ref[:, cols] = _head_rms(y[:, c:c + HEAD_DIM], cg_ref[:, cols])


def _norm_matmul(x, gain, w, *, mode, out_dtype, tiles, layer=0, col_gain=None, emit_w=False):
    m, k = x.shape
    n = w.shape[-1]
    tm, tn = tiles.tm, tiles.tn_full_k
    fused_norm = gain is not None
    assert not emit_w or m == tm
    assert fused_norm or x.dtype == BF16
    if col_gain is None:
        col_gain = jnp.ones((1, n), F32)
    body = functools.partial(_norm_matmul_body, mode=mode, tm=tm, tn=tn, emit_w=emit_w, fused_norm=fused_norm)
    out_specs = pl.BlockSpec((tm, tn), lambda i, j: (i, j))
    out_shape = jax.ShapeDtypeStruct((m, n), out_dtype)
    if emit_w:
        out_specs = [out_specs, pl.BlockSpec((k, tn), lambda i, j: (0, j))]
        out_shape = [out_shape, jax.ShapeDtypeStruct((k, n), BF16)]
    if fused_norm:
        x_specs = [pl.BlockSpec((tm, k), lambda i, j: (i, 0), pipeline_mode=pl.Buffered(1)),
                   pl.BlockSpec((1, k), lambda i, j: (0, 0))]
        x_args = (x, gain.reshape(1, k))
        scratch = [pltpu.VMEM((tm, k), BF16)]
    else:
        x_specs = [pl.BlockSpec((tm, k), lambda i, j: (i, 0))]
        x_args = (x,)
        scratch = []
    return pl.pallas_call(
        body,
        grid=(m // tm, n // tn),
        in_specs=x_specs + [
            _weight_spec(w, layer, (k, tn), lambda i, j: (0, j)),
            pl.BlockSpec((1, tn), lambda i, j: (0, j)),
        ],
        out_specs=out_specs,
        out_shape=out_shape,
        scratch_shapes=scratch,
        compiler_params=_params("arbitrary", "arbitrary"),
        name=f"norm_matmul_{mode}_{tm}",
    )(*x_args, w, col_gain)


def _matmul_res_body(a_ref, w_ref, r_ref, o_ref, *rest, emit_w):
    kk = pl.program_id(2)
    if emit_w:
        (wb_ref,) = rest
        wb_ref[...] = w_ref[...].astype(BF16)
        w_src = wb_ref
    else:
        w_src = w_ref
    d = jnp.dot(a_ref[...], w_src[...], preferred_element_type=F32)

    @pl.when(kk == 0)
    def _():
        o_ref[...] = r_ref[...] + d

    @pl.when(kk != 0)
    def _():
        o_ref[...] += d


def _matmul_res(a, w, res, *, tiles, layer=0, emit_w=False):
    m, k = a.shape
    n = w.shape[-1]
    tm, tn, tk = tiles.tm, tiles.tn_res, min(tiles.tk_res, k)
    assert not emit_w or m == tm
    out_specs = pl.BlockSpec((tm, tn), lambda i, j, kk: (i, j))
    out_shape = jax.ShapeDtypeStruct((m, n), F32)
    if emit_w:
        out_specs = [out_specs, pl.BlockSpec((tk, tn), lambda i, j, kk: (kk, j))]
        out_shape = [out_shape, jax.ShapeDtypeStruct((k, n), BF16)]
    return pl.pallas_call(
        functools.partial(_matmul_res_body, emit_w=emit_w),
        grid=(m // tm, n // tn, k // tk),
        in_specs=[
            pl.BlockSpec((tm, tk), lambda i, j, kk: (i, kk)),
            _weight_spec(w, layer, (tk, tn), lambda i, j, kk: (kk, j)),
            pl.BlockSpec((tm, tn), lambda i, j, kk: (i, j)),
        ],
        out_specs=out_specs,
        out_shape=out_shape,
        compiler_params=_params("arbitrary", "arbitrary", "arbitrary"),
        name=f"matmul_res_{tm}_{k}",
    )(a, w, res)


def _rel_bucket(dist):
    max_exact = N_BUCKETS // 2
    d32 = jnp.maximum(dist, 1).astype(F32)
    large = max_exact + (jnp.log(d32 / max_exact) / math.log(MAX_DISTANCE / max_exact)
                         * (N_BUCKETS - max_exact)).astype(jnp.int32)
    large = jnp.minimum(large, N_BUCKETS - 1)
    return jnp.where(dist < max_exact, dist, large)


def _bias_by_stride(bias_table):
    m = jnp.arange(BAND + 1)
    return jnp.stack([bias_table[:, g][_rel_bucket(m * d)] for g, (_, d) in enumerate(DILATED_PAIRS)])


def _prompt_bias(bias_m):
    period = 2 * BAND + 1
    u = jnp.concatenate([bias_m[:, ::-1, :], jnp.full((N_GROUPS, period - BAND - 1, N_A_HEADS), MASKED, F32)],
                        axis=1).transpose(0, 2, 1)
    flat = jnp.tile(u, (1, 1, BAND))[:, :, :BAND * 2 * BAND]
    return flat.reshape(N_GROUPS, N_A_HEADS, BAND, 2 * BAND)


def _attn_a_prompt_body(q0, k0, v0, q1, k1, v1, q2, k2, v2, bias_ref, o_ref, og_ref, lse_ref, *, t):
    qkv = ((q0, k0, v0), (q1, k1, v1), (q2, k2, v2))
    ones = jnp.ones((BAND, HEAD_DIM), BF16)
    dn = (((1,), (1,)), ((), ()))

    def rows_of(start, d):
        if d == 1:
            return pl.ds(start, BAND)
        return pl.ds(start, BAND, stride=d)

    def blocks(g, specs):
        q_ref, k_ref, v_ref = qkv[g]
        with_prev = specs[0][1] is not None

        def logits(q, rows, bias):
            return lax.dot_general(q, k_ref[rows, :].astype(BF16), dn, preferred_element_type=F32) * ATT_SCALE + bias

        def weighted(p, rows):
            vx = jnp.concatenate([v_ref[rows, :].astype(BF16), ones], axis=1)
            return jnp.dot(p.astype(BF16), vx, preferred_element_type=F32)

        qs = [q_ref[cur, :].astype(BF16) for cur, _ in specs]
        s_cs = [logits(q, cur, bias_ref[g, :, BAND:]) for q, (cur, _) in zip(qs, specs)]
        mxs = [jnp.max(s, axis=-1, keepdims=True) for s in s_cs]
        if with_prev:
            s_ps = [logits(q, prev, bias_ref[g, :, :BAND]) for q, (_, prev) in zip(qs, specs)]
            mxs = [jnp.maximum(mx, jnp.max(s, axis=-1, keepdims=True)) for mx, s in zip(mxs, s_ps)]
        accs = [weighted(jnp.exp(s - mx), cur) for s, mx, (cur, _) in zip(s_cs, mxs, specs)]
        if with_prev:
            accs = [acc + weighted(jnp.exp(s - mx), prev) for acc, s, mx, (_, prev) in zip(accs, s_ps, mxs, specs)]
        for acc, mx, (cur, _) in zip(accs, mxs, specs):
            den = acc[:, HEAD_DIM:]
            og_ref[g, cur, :] = acc[:, :HEAD_DIM] / den
            lse_ref[g, cur, :] = mx + jnp.log(den)

    for g, (_, d) in enumerate(DILATED_PAIRS):
        span = BAND * d
        n_blk = t // span
        if n_blk == 1:
            per = 8

            def residue_step(it, carry, g=g, d=d, per=per):
                blocks(g, [(rows_of(it * per + u, d), None) for u in range(per)])
                return carry
            lax.fori_loop(0, d // per, residue_step, 0)
        elif d == 1:
            per = 5
            assert (n_blk - 1) % per == 0
            blocks(g, [(rows_of(0, d), None)])

            def blk_step(it, carry, g=g, d=d, per=per, span=span):
                starts = [pl.multiple_of((1 + it * per + u) * span, BAND) for u in range(per)]
                blocks(g, [(rows_of(st, d), rows_of(st - span, d)) for st in starts])
                return carry
            lax.fori_loop(0, (n_blk - 1) // per, blk_step, 0)
        else:
            blocks(g, [(rows_of(r, d), None) for r in range(d)])

            def blk_step(n, carry, g=g, d=d, span=span):
                blocks(g, [(rows_of(n * span + r, d), rows_of(n * span + r - span, d)) for r in range(d)])
                return carry
            lax.fori_loop(1, n_blk, blk_step, 0)

    def merge_step(c, carry):
        rows = pl.ds(pl.multiple_of(c * BAND, BAND), BAND)
        l0, l1, l2 = lse_ref[0, rows, :], lse_ref[1, rows, :], lse_ref[2, rows, :]
        mx = jnp.maximum(jnp.maximum(l0, l1), l2)
        e0, e1, e2 = jnp.exp(l0 - mx), jnp.exp(l1 - mx), jnp.exp(l2 - mx)
        tot = e0 + e1 + e2
        o = (e0 / tot) * og_ref[0, rows, :] + (e1 / tot) * og_ref[1, rows, :] + (e2 / tot) * og_ref[2, rows, :]
        o_ref[rows, :] = o.astype(o_ref.dtype)
        return carry

    lax.fori_loop(0, t // BAND, merge_step, 0)


def _attn_a_prompt(qkv, bias, *, b, t):
    heads = N_A_HEADS

    def col_spec(g, c):
        base = (g * 3 + c) * heads
        return pl.BlockSpec((t, HEAD_DIM), lambda bi, h: (bi, base + h))

    in_specs = [col_spec(g, c) for g in range(N_GROUPS) for c in range(3)]
    in_specs.append(pl.BlockSpec((N_GROUPS, None, BAND, 2 * BAND), lambda bi, h: (0, h, 0, 0)))
    return pl.pallas_call(
        functools.partial(_attn_a_prompt_body, t=t),
        grid=(b, heads),
        in_specs=in_specs,
        out_specs=pl.BlockSpec((t, HEAD_DIM), lambda bi, h: (bi, h)),
        out_shape=jax.ShapeDtypeStruct((b * t, A_WIDTH), BF16),
        scratch_shapes=[pltpu.VMEM((N_GROUPS, t, HEAD_DIM), F32),
                        pltpu.VMEM((N_GROUPS, t, HEAD_DIM), F32)],
        compiler_params=_params("arbitrary", "arbitrary"),
        name="attn_a_prompt",
    )(*([qkv] * 9), bias)


def _attn_a_sample_body(qkv_ref, c0_ref, c1_ref, c2_ref, bb0_ref, bb12_ref, bn0_ref, bn12_ref, o_ref):
    s = pl.program_id(1)
    outs, lses = [], []
    for g in range(N_GROUPS):
        q = qkv_ref[s, g, 0]
        if g == 0:
            kb, vb, bias_b = c0_ref[:, 0], c0_ref[:, 1], bb0_ref[...]
            kn, vn, bias_n = qkv_ref[:, 0, 1], qkv_ref[:, 0, 2], bn0_ref[...]
        else:
            c_ref = c1_ref if g == 1 else c2_ref
            kb, vb, bias_b = c_ref[:, 0], c_ref[:, 1], bb12_ref[g - 1]
            kn, vn, bias_n = qkv_ref[pl.ds(s, 1), g, 1], qkv_ref[pl.ds(s, 1), g, 2], bn12_ref[g - 1][None]
        lb = jnp.sum(q[None] * kb, axis=-1, keepdims=True) * ATT_SCALE + bias_b
        ln = jnp.sum(q[None] * kn, axis=-1, keepdims=True) * ATT_SCALE + bias_n
        mx = jnp.maximum(jnp.max(lb, axis=0), jnp.max(ln, axis=0))
        pb = jnp.exp(lb - mx[None])
        pn = jnp.exp(ln - mx[None])
        den = jnp.sum(pb, axis=0) + jnp.sum(pn, axis=0)
        acc = jnp.sum(pb * vb, axis=0) + jnp.sum(pn * vn, axis=0)
        outs.append(acc / den)
        lses.append(mx + jnp.log(den))
    mx = jnp.maximum(jnp.maximum(lses[0], lses[1]), lses[2])
    es = [jnp.exp(l - mx) for l in lses]
    tot = es[0] + es[1] + es[2]
    o_ref[...] = (es[0] / tot) * outs[0] + (es[1] / tot) * outs[1] + (es[2] / tot) * outs[2]


def _attn_a_sample(qkv_s, caches, bias_m, *, bs, s_new):
    h = N_A_HEADS
    lane = (h, HEAD_DIM)

    def dense(x):
        return jnp.broadcast_to(x[..., None], x.shape + (HEAD_DIM,)).astype(F32)

    sq = jnp.arange(s_new)
    r = jnp.arange(BAND)
    m_b = BAND + sq[:, None] - r[None, :]
    bb0 = dense(jnp.where((m_b <= BAND)[..., None], bias_m[0][jnp.clip(m_b, 0, BAND)], MASKED))
    m_n = sq[:, None] - sq[None, :]
    bn0 = dense(jnp.where((m_n >= 0)[..., None], bias_m[0][jnp.clip(m_n, 0, BAND)], MASKED))
    bb12 = dense(jnp.stack([bias_m[g][BAND - r] for g in (1, 2)]))
    bn12 = dense(jnp.stack([bias_m[g][0] for g in (1, 2)]))

    c0 = caches[0]
    c1 = caches[1].reshape(bs, BAND, DILATED_PAIRS[1][1], 2, h, HEAD_DIM)
    c2 = caches[2].reshape(bs, BAND, DILATED_PAIRS[2][1], 2, h, HEAD_DIM)
    return pl.pallas_call(
        _attn_a_sample_body,
        grid=(bs, s_new),
        in_specs=[
            pl.BlockSpec((None, s_new, N_GROUPS, 3) + lane, lambda b, s: (b, 0, 0, 0, 0, 0)),
            pl.BlockSpec((None, BAND, 2) + lane, lambda b, s: (b, 0, 0, 0, 0)),
            pl.BlockSpec((None, BAND, None, 2) + lane, lambda b, s: (b, 0, s, 0, 0, 0)),
            pl.BlockSpec((None, BAND, None, 2) + lane, lambda b, s: (b, 0, s, 0, 0, 0)),
            pl.BlockSpec((None, BAND) + lane, lambda b, s: (s, 0, 0, 0)),
            pl.BlockSpec((2, BAND) + lane, lambda b, s: (0, 0, 0, 0)),
            pl.BlockSpec((None, s_new) + lane, lambda b, s: (s, 0, 0, 0)),
            pl.BlockSpec((2,) + lane, lambda b, s: (0, 0, 0)),
        ],
        out_specs=pl.BlockSpec((None, None) + lane, lambda b, s: (b, s, 0, 0)),
        out_shape=jax.ShapeDtypeStruct((bs, s_new) + lane, F32),
        compiler_params=_params("arbitrary", "arbitrary"),
        name="attn_a_sample",
    )(qkv_s, c0, c1, c2, bb0, bb12, bn0, bn12)


LOG2E = 1.0 / math.log(2.0)
SB_LOGIT_SCALE = ATT_SCALE * LOG2E


def _softplus2(z2):
    return jnp.maximum(z2, 0.0) + jnp.log2(1.0 + jnp.exp2(-jnp.abs(z2)))


def _suffix_matrix():
    n = 2 * PAGE_SIZE
    j = lax.broadcasted_iota(jnp.int32, (n, n), 0) % PAGE_SIZE
    c = lax.broadcasted_iota(jnp.int32, (n, n), 1)
    return jnp.where((c >= PAGE_SIZE) | (j > c), -1.0, 0.0).astype(BF16)


def _hi_lo(l):
    hi = l.astype(BF16)
    lo = (l - hi.astype(F32)).astype(BF16)
    return jnp.concatenate([hi, lo], axis=1)


def _attn_b_prompt_body(q_ref, k_ref, v_ref, bias_ref, o_ref, kb_ref, vb_ref, *, t, tq):
    i = pl.program_id(2)

    @pl.when(i == 0)
    def _():
        def cast_step(c, carry):
            rows = pl.ds(pl.multiple_of(c * 256, 256), 256)
            kb_ref[rows, :] = (k_ref[rows, :] * SB_LOGIT_SCALE).astype(BF16)
            vb_ref[rows, :] = v_ref[rows, :].astype(BF16)
            return carry
        lax.fori_loop(0, t // 256, cast_step, 0)

    suffix = _suffix_matrix()
    dn = (((1,), (1,)), ((), ()))
    heads = range(GQ)

    def block_rows(jlast, nblk):
        return pl.ds(pl.multiple_of((jlast - (nblk - 1)) * PAGE_SIZE, PAGE_SIZE), nblk * PAGE_SIZE)

    def logits(rows, nblk):
        k = kb_ref[rows, :]
        return [lax.dot_general(q_ref[:, g * HEAD_DIM:(g + 1) * HEAD_DIM], k, dn, preferred_element_type=F32)
                + bias_ref[g, :, :nblk * PAGE_SIZE] for g in heads]

    def suffix_sums(zs, nblk, visible):
        sps = [_softplus2(z) for z in zs]
        ls = [sp if visible is None else jnp.where(visible, sp, 0.0) for sp in sps]
        ss = [jnp.dot(jnp.concatenate([_hi_lo(l[:, u * PAGE_SIZE:(u + 1) * PAGE_SIZE])
                                       for u in reversed(range(nblk))], axis=0),
                      suffix, preferred_element_type=F32) for l in ls]
        return sps, ss

    def weights(zs, sps, ss, runs, nblk, visible):
        avals, new_runs = [], []
        for g, (z, sp, s) in enumerate(zip(zs, sps, ss)):
            run = None if runs is None else runs[g]
            later = [None] * nblk
            for idx in range(nblk):
                su = s[idx * tq:(idx + 1) * tq]
                later[nblk - 1 - idx] = su[:, :PAGE_SIZE] if run is None else su[:, :PAGE_SIZE] + run
                run = su[:, PAGE_SIZE:] if run is None else run + su[:, PAGE_SIZE:]
            a = jnp.exp2(z - sp + (later[0] if nblk == 1 else jnp.concatenate(later, axis=1)))
            if visible is not None:
                a = jnp.where(visible, a, 0.0)
            avals.append(a.astype(BF16))
            new_runs.append(run)
        return avals, tuple(new_runs)

    def weighted_values(avals, rows):
        v = vb_ref[rows, :]
        return [jnp.dot(a, v, preferred_element_type=F32) for a in avals]

    def sweep(jlast, nblk, state):
        rows = block_rows(jlast, nblk)
        visible = None
        if state is None:
            qpos = lax.broadcasted_iota(jnp.int32, (tq, nblk * PAGE_SIZE), 0)
            kpos = lax.broadcasted_iota(jnp.int32, (tq, nblk * PAGE_SIZE), 1) - (nblk - 1) * PAGE_SIZE
            visible = kpos < qpos
        zs = logits(rows, nblk)
        sps, ss = suffix_sums(zs, nblk, visible)
        avals, runs = weights(zs, sps, ss, None if state is None else state[0], nblk, visible)
        pvs = weighted_values(avals, rows)
        accs = pvs if state is None else [acc + pv for acc, pv in zip(state[1], pvs)]
        return runs, tuple(accs)

    total = i + 1
    n_groups = lax.shift_right_logical(total * 11, 5)
    rem = total - n_groups * KEY_BLOCKS_PER_SWEEP
    state = lax.cond(rem == 1, lambda: sweep(i, 1, None),
                     lambda: lax.cond(rem == 2, lambda: sweep(i, 2, None),
                                      lambda: sweep(i, KEY_BLOCKS_PER_SWEEP, None)))
    first_size = jnp.where(rem == 0, KEY_BLOCKS_PER_SWEEP, rem)
    n_rest = n_groups - jnp.where(rem == 0, 1, 0)
    first = i - first_size
    _, accs = lax.fori_loop(0, n_rest,
                            lambda p, st: sweep(first - KEY_BLOCKS_PER_SWEEP * p, KEY_BLOCKS_PER_SWEEP, st), state)
    for g in heads:
        o_ref[:, g * HEAD_DIM:(g + 1) * HEAD_DIM] = accs[g].astype(o_ref.dtype)


def _attn_b_prompt(q, kv, sb, *, b, t):
    tq = PAGE_SIZE
    nq = t // tq
    assert KEY_BLOCKS_PER_SWEEP == 3 and nq <= 16
    bias = jnp.broadcast_to(sb[:, :, None, None], (N_KV_HEADS, GQ, 1, KEY_BLOCKS_PER_SWEEP * PAGE_SIZE))
    return pl.pallas_call(
        functools.partial(_attn_b_prompt_body, t=t, tq=tq),
        grid=(b, N_KV_HEADS, nq),
        in_specs=[
            pl.BlockSpec((tq, GQ * HEAD_DIM), lambda bi, kh, i: (bi * nq + i, kh)),
            pl.BlockSpec((t, HEAD_DIM), lambda bi, kh, i: (bi, kh)),
            pl.BlockSpec((t, HEAD_DIM), lambda bi, kh, i: (bi, N_KV_HEADS + kh)),
            pl.BlockSpec((None, GQ, 1, KEY_BLOCKS_PER_SWEEP * PAGE_SIZE), lambda bi, kh, i: (kh, 0, 0, 0)),
        ],
        out_specs=pl.BlockSpec((tq, GQ * HEAD_DIM), lambda bi, kh, i: (bi * nq + i, kh)),
        out_shape=jax.ShapeDtypeStruct((b * t, N_B_HEADS * HEAD_DIM), BF16),
        scratch_shapes=[pltpu.VMEM((t, HEAD_DIM), BF16), pltpu.VMEM((t, HEAD_DIM), BF16)],
        compiler_params=_params("arbitrary", "arbitrary", "arbitrary"),
        name="attn_b_prompt",
    )(q, kv, kv, bias)


def _attn_b_sample_body(pt_ref, q_ref, kvn_ref, bias_ref, *rest, s_new, n_steps):
    page_refs = rest[:PAGES_PER_STEP]
    o_ref, run_ref, acc_ref = rest[PAGES_PER_STEP:]
    p = pl.program_id(1)
    rows = GQ * s_new
    row_tok = lax.broadcasted_iota(jnp.int32, (rows, HEAD_DIM), 0) % s_new

    @pl.when(p == 0)
    def _():
        for kh in range(N_KV_HEADS):
            q = q_ref[kh]
            run = jnp.zeros((rows, HEAD_DIM), F32)
            acc = jnp.zeros((rows, HEAD_DIM), F32)
            for j in reversed(range(s_new)):
                kj = kvn_ref[0, kh, pl.ds(j, 1), :]
                vj = kvn_ref[1, kh, pl.ds(j, 1), :]
                z = jnp.sum(q * kj, axis=-1, keepdims=True) * SB_LOGIT_SCALE + bias_ref[kh]
                valid = row_tok > j
                sp = _softplus2(z)
                a = jnp.where(valid, jnp.exp2(z - sp + run), 0.0)
                acc = acc + a * vj
                run = run + jnp.where(valid, -sp, 0.0)
            run_ref[kh] = run
            acc_ref[kh] = acc

    suffix = _suffix_matrix()
    stride = 2 * N_KV_HEADS
    dn = (((1,), (1,)), ((), ()))
    def head_rows(first):
        return jnp.concatenate([r[pl.ds(first, PAGE_SIZE, stride=stride), :] for r in page_refs], axis=0).astype(BF16)

    def later_and_run(s, run):
        later = []
        for u in range(PAGES_PER_STEP):
            su = s[u * rows:(u + 1) * rows]
            later.append(su[:, :PAGE_SIZE] + run)
            run = run + su[:, PAGE_SIZE:]
        return jnp.concatenate(later, axis=1), run

    heads = range(N_KV_HEADS)
    zs = [lax.dot_general(q_ref[kh].astype(BF16), head_rows(kh), dn, preferred_element_type=F32) * SB_LOGIT_SCALE
          + jnp.tile(bias_ref[kh], (1, PAGES_PER_STEP)) for kh in heads]
    sps = [_softplus2(z) for z in zs]
    ss = [jnp.dot(jnp.concatenate([_hi_lo(sp[:, u * PAGE_SIZE:(u + 1) * PAGE_SIZE])
                                   for u in range(PAGES_PER_STEP)], axis=0),
                  suffix, preferred_element_type=F32) for sp in sps]
    lrs = [later_and_run(s, run_ref[kh]) for kh, s in zip(heads, ss)]
    avals = [jnp.exp2(z - sp + later) for z, sp, (later, _) in zip(zs, sps, lrs)]
    pvs = [jnp.dot(a.astype(BF16), head_rows(N_KV_HEADS + kh), preferred_element_type=F32)
           for kh, a in zip(heads, avals)]
    for kh in heads:
        acc_ref[kh] += pvs[kh]
        run_ref[kh] = lrs[kh][1]

    @pl.when(p == n_steps - 1)
    def _():
        o_ref[...] = acc_ref[...]


def _attn_b_sample(q_rows, kv_new, sb, cache_kv, page_table, *, s_new):
    bs, n_pages = page_table.shape
    assert n_pages % PAGES_PER_STEP == 0
    n_steps = n_pages // PAGES_PER_STEP
    rows = GQ * s_new
    bias = jnp.broadcast_to(sb[:, :, None, None], (N_KV_HEADS, GQ, s_new, HEAD_DIM)).reshape(
        N_KV_HEADS, rows, HEAD_DIM)
    pool = cache_kv.reshape(cache_kv.shape[0], PAGE_SIZE * 2 * N_KV_HEADS, HEAD_DIM)

    def page_spec(u):
        return pl.BlockSpec((None, PAGE_SIZE * 2 * N_KV_HEADS, HEAD_DIM),
                            lambda b, p, pt: (pt[b, n_pages - 1 - (p * PAGES_PER_STEP + u)], 0, 0))

    grid_spec = pltpu.PrefetchScalarGridSpec(
        num_scalar_prefetch=1,
        grid=(bs, n_steps),
        in_specs=[
            pl.BlockSpec((None, N_KV_HEADS, rows, HEAD_DIM), lambda b, p, pt: (b, 0, 0, 0)),
            pl.BlockSpec((None, 2, N_KV_HEADS, s_new, HEAD_DIM), lambda b, p, pt: (b, 0, 0, 0, 0)),
            pl.BlockSpec((N_KV_HEADS, rows, HEAD_DIM), lambda b, p, pt: (0, 0, 0)),
        ] + [page_spec(u) for u in range(PAGES_PER_STEP)],
        out_specs=pl.BlockSpec((None, N_KV_HEADS, rows, HEAD_DIM), lambda b, p, pt: (b, 0, 0, 0)),
        scratch_shapes=[pltpu.VMEM((N_KV_HEADS, rows, HEAD_DIM), F32),
                        pltpu.VMEM((N_KV_HEADS, rows, HEAD_DIM), F32)],
    )
    return pl.pallas_call(
        functools.partial(_attn_b_sample_body, s_new=s_new, n_steps=n_steps),
        grid_spec=grid_spec,
        out_shape=jax.ShapeDtypeStruct((bs, N_KV_HEADS, rows, HEAD_DIM), F32),
        compiler_params=_params("arbitrary", "arbitrary"),
        name="attn_b_sample",
    )(page_table, q_rows, kv_new, bias, *([pool] * PAGES_PER_STEP))


def _roll_rows(buf, new):
    wb, s = buf.shape[1], new.shape[1]
    zero = jnp.zeros((), buf.dtype)
    shift_cfg = [(0, 0, 0)] * buf.ndim
    shift_cfg[1] = (-s, s, 0)
    new_cfg = [(0, 0, 0)] * buf.ndim
    new_cfg[1] = (wb - s, 0, 0)
    row = lax.broadcasted_iota(jnp.int32, buf.shape, 1)
    return jnp.where(row >= wb - s, lax.pad(new, zero, new_cfg), lax.pad(buf, zero, shift_cfg))


def _mlp(x, gain, w_up, w_down, *, tiles, layer=0, emit_w=False):
    if emit_w:
        h, w_up_b = _norm_matmul(x, gain, w_up, mode="relu2", out_dtype=BF16, tiles=tiles, layer=layer, emit_w=True)
        y, w_down_b = _matmul_res(h, w_down, x, tiles=tiles, layer=layer, emit_w=True)
        return y, w_up_b, w_down_b
    (xn,) = _rms_norm(x, [gain], tr=tiles.tr_norm)
    h = _norm_matmul(xn, None, w_up, mode="relu2", out_dtype=BF16, tiles=tiles)
    return _matmul_res(h, w_down, x, tiles=tiles)


@jax.jit
def _step(x_prompt, x_sample, cache_win_g0, cache_win_g1, cache_win_g2, cache_kv, page_table, bias_table,
          norm_mix, norm_mlp, w_a_in, g_a_q, g_a_k, w_a_out, norm_kv, w_kv, w_b_q, b_sb, w_b_out, w_up, w_down):
    bp, tp, _ = x_prompt.shape
    bs, ts, _ = x_sample.shape
    pt, st = PROMPT_TILES, _sample_tiles(bs * ts)
    xp = x_prompt.reshape(bp * tp, D_MODEL)
    xs = x_sample.reshape(bs * ts, D_MODEL)
    caches = (cache_win_g0[0], cache_win_g1[0], cache_win_g2[0])

    head_gain = jnp.concatenate([jnp.tile(g_a_q[0], N_A_HEADS), jnp.tile(g_a_k[0], N_A_HEADS),
                                 jnp.ones((A_WIDTH,), F32)])
    col_gain = jnp.tile(head_gain, N_GROUPS).reshape(1, -1).astype(F32)
    qkv_s, w_in = _norm_matmul(xs, norm_mix[0], w_a_in, mode="qkv", out_dtype=F32, tiles=st,
                               col_gain=col_gain, emit_w=True)
    (xn_p,) = _rms_norm(xp, [norm_mix[0]], tr=pt.tr_norm)
    qkv_p = _norm_matmul(xn_p, None, w_in, mode="qkv", out_dtype=F32, tiles=pt, col_gain=col_gain)

    bias_m = _bias_by_stride(bias_table)
    qkv_s6 = qkv_s.reshape(bs, ts, N_GROUPS, 3, N_A_HEADS, HEAD_DIM)
    o_s = _attn_a_sample(qkv_s6, caches, bias_m, bs=bs, s_new=ts)
    o_s = o_s.reshape(bs * ts, A_WIDTH).astype(BF16)
    o_p = _attn_a_prompt(qkv_p, _prompt_bias(bias_m), b=bp, t=tp)

    xs, w_ao = _matmul_res(o_s, w_a_out, xs, tiles=st, emit_w=True)
    xp = _matmul_res(o_p, w_ao, xp, tiles=pt)
    xs, w_up0, w_dn0 = _mlp(xs, norm_mlp[0], w_up, w_down, tiles=st, layer=0, emit_w=True)
    xp = _mlp(xp, norm_mlp[0], w_up0, w_dn0, tiles=pt)

    kv_s, w_kvb = _norm_matmul(xs, norm_kv, w_kv, mode="plain", out_dtype=F32, tiles=st, emit_w=True)
    xn_kv, xn_bq = _rms_norm(xp, [norm_kv, norm_mix[1]], tr=pt.tr_norm)
    kv_p = _norm_matmul(xn_kv, None, w_kvb, mode="plain", out_dtype=F32, tiles=pt)

    qb_s, w_bq = _norm_matmul(xs, norm_mix[1], w_b_q, mode="plain", out_dtype=F32, tiles=st, emit_w=True)
    qb_p = _norm_matmul(xn_bq, None, w_bq, mode="plain", out_dtype=BF16, tiles=pt)

    sb = b_sb[0].astype(F32).reshape(N_KV_HEADS, GQ) * LOG2E
    q_rows = qb_s.reshape(bs, ts, N_KV_HEADS, GQ, HEAD_DIM).transpose(0, 2, 3, 1, 4).reshape(
        bs, N_KV_HEADS, GQ * ts, HEAD_DIM)
    kv_s5 = kv_s.reshape(bs, ts, 2, N_KV_HEADS, HEAD_DIM)
    ob_s = _attn_b_sample(q_rows, kv_s5.transpose(0, 2, 3, 1, 4), sb, cache_kv, page_table, s_new=ts)
    ob_s = ob_s.reshape(bs, N_KV_HEADS, GQ, ts, HEAD_DIM).transpose(0, 3, 1, 2, 4).reshape(
        bs * ts, N_B_HEADS * HEAD_DIM).astype(BF16)
    ob_p = _attn_b_prompt(qb_p, kv_p, sb, b=bp, t=tp)

    xs, w_bo = _matmul_res(ob_s, w_b_out, xs, tiles=st, emit_w=True)
    xp = _matmul_res(ob_p, w_bo, xp, tiles=pt)
    xs, w_up1, w_dn1 = _mlp(xs, norm_mlp[1], w_up, w_down, tiles=st, layer=1, emit_w=True)
    xp = _mlp(xp, norm_mlp[1], w_up1, w_dn1, tiles=pt)

    qkv_p6 = qkv_p.reshape(bp, tp, N_GROUPS, 3, N_A_HEADS, HEAD_DIM)
    win_p = [qkv_p6[:, tp - min(window, tp):, g, 1:3][None] for g, (window, _) in enumerate(DILATED_PAIRS)]
    win_s = [_roll_rows(caches[g], qkv_s6[:, :, g, 1:3])[None] for g in range(N_GROUPS)]
    return (xp.reshape(bp, tp, D_MODEL), xs.reshape(bs, ts, D_MODEL), *win_p, *win_s,
            kv_p.reshape(bp, tp, 2, N_KV_HEADS, HEAD_DIM), kv_s5)


def kernel(x_prompt, x_sample, cache_win_g0, cache_win_g1, cache_win_g2, cache_kv, page_table, bias_table,
           norm_mix, norm_mlp, w_a_in, g_a_q, g_a_k, w_a_out, norm_kv, w_kv, w_b_q, b_sb, w_b_out, w_up, w_down):
    return _step(x_prompt, x_sample, cache_win_g0, cache_win_g1, cache_win_g2, cache_kv, page_table, bias_table,
                 norm_mix, norm_mlp, w_a_in, g_a_q, g_a_k, w_a_out, norm_kv, w_kv, w_b_q, b_sb, w_b_out,
                 w_up, w_down)
```

```python
import functools
import math
from typing import NamedTuple

import jax
import jax.numpy as jnp
from jax import lax
from jax.experimental import pallas as pl
from jax.experimental.pallas import tpu as pltpu

F32 = jnp.float32
BF16 = jnp.bfloat16

D_MODEL = 4096
HEAD_DIM = 128
DILATED_PAIRS = ((128, 1), (512, 4), (2048, 16))
N_GROUPS = len(DILATED_PAIRS)
N_A_HEADS = D_MODEL // (2 * HEAD_DIM)
A_WIDTH = N_A_HEADS * HEAD_DIM
N_B_HEADS = D_MODEL // HEAD_DIM
N_KV_HEADS = 8
GQ = N_B_HEADS // N_KV_HEADS
N_BUCKETS = 32
MAX_DISTANCE = 2048
RMS_EPS = 1e-6
PAGE_SIZE = 128
BAND = 128
ATT_SCALE = HEAD_DIM ** -0.5
MASKED = -1e30

V7X_VMEM_LIMIT_BYTES = 60 * 1024 * 1024
V7X_MXU_COLS = 256
NORM_ROWS = 16
PAGES_PER_STEP = 4
KEY_BLOCKS_PER_SWEEP = 3


class Tiles(NamedTuple):
    tm: int
    tn_full_k: int
    tn_res: int
    tk_res: int
    tr_norm: int


PROMPT_TILES = Tiles(tm=1024, tn_full_k=1024, tn_res=1024, tk_res=4096, tr_norm=512)


def _sample_tiles(rows):
    return Tiles(tm=rows, tn_full_k=512, tn_res=1024, tk_res=2048, tr_norm=rows)


def _params(*semantics):
    return pltpu.CompilerParams(dimension_semantics=semantics,
                                vmem_limit_bytes=V7X_VMEM_LIMIT_BYTES)


def _weight_spec(w, layer, block, index):
    if w.ndim == 2:
        return pl.BlockSpec(block, index)
    return pl.BlockSpec((None,) + block, lambda *g: (layer,) + index(*g))


def _head_rms(t, cg):
    ms = jnp.mean(t * t, axis=-1, keepdims=True)
    return t * lax.rsqrt(ms + RMS_EPS) * cg


def _rms_rows(x_ref, g_refs, o_refs, n_rows):
    per = math.gcd(n_rows // NORM_ROWS, 4)

    def rows_step(c, carry):
        groups = [pl.ds(pl.multiple_of((c * per + u) * NORM_ROWS, NORM_ROWS), NORM_ROWS) for u in range(per)]
        scales = [lax.rsqrt(jnp.mean(jnp.square(x_ref[rows, :]), axis=-1, keepdims=True) + RMS_EPS)
                  for rows in groups]
        for rows, scale in zip(groups, scales):
            y = x_ref[rows, :] * scale
            for g_ref, o_ref in zip(g_refs, o_refs):
                o_ref[rows, :] = (y * g_ref[...]).astype(BF16)
        return carry

    lax.fori_loop(0, n_rows // (NORM_ROWS * per), rows_step, 0)


def _rms_norm_body(x_ref, *refs, n_gains, tr):
    _rms_rows(x_ref, refs[:n_gains], refs[n_gains:], tr)


def _rms_norm(x, gains, *, tr):
    m, k = x.shape
    n = len(gains)
    outs = pl.pallas_call(
        functools.partial(_rms_norm_body, n_gains=n, tr=tr),
        grid=(m // tr,),
        in_specs=[pl.BlockSpec((tr, k), lambda i: (i, 0))] + [pl.BlockSpec((1, k), lambda i: (0, 0))] * n,
        out_specs=[pl.BlockSpec((tr, k), lambda i: (i, 0))] * n,
        out_shape=[jax.ShapeDtypeStruct((m, k), BF16)] * n,
        compiler_params=_params("arbitrary"),
        name=f"rms_norm_{n}",
    )(x, *[g.reshape(1, k) for g in gains])
    return outs


def _norm_matmul_body(x_ref, *refs, mode, tm, tn, emit_w, fused_norm):
    if fused_norm:
        g_ref, w_ref, cg_ref, o_ref, *rest = refs
    else:
        w_ref, cg_ref, o_ref, *rest = refs
    if emit_w:
        wb_ref, *rest = rest
    j = pl.program_id(1)

    if fused_norm:
        (xn_ref,) = rest

        @pl.when(j == 0)
        def _():
            _rms_rows(x_ref, [g_ref], [xn_ref], tm)
    else:
        xn_ref = x_ref

    if emit_w:
        wb_ref[...] = w_ref[...].astype(BF16)
        w_src = wb_ref
    else:
        w_src = w_ref

    if mode == "relu2":
        r = jnp.maximum(jnp.dot(xn_ref[...], w_src[...], preferred_element_type=F32), 0.0)
        o_ref[...] = (r * r).astype(o_ref.dtype)
    elif mode == "plain":
        o_ref[...] = jnp.dot(xn_ref[...], w_src[...], preferred_element_type=F32).astype(o_ref.dtype)
    else:
        kind = (j * tn // A_WIDTH) % 3

        @pl.when(kind == 2)
        def _():
            o_ref[...] = jnp.dot(xn_ref[...], w_src[...], preferred_element_type=F32)

        @pl.when(kind != 2)
        def _():
            for c0 in range(0, tn, V7X_MXU_COLS):
                y = jnp.dot(xn_ref[...], w_src[:, c0:c0 + V7X_MXU_COLS], preferred_element_type=F32)
                for c in range(0, V7X_MXU_COLS, HEAD_DIM):
                    cols = slice(c0 + c, c0 + c + HEAD_DIM)
                    o_ref[:, cols] = _head_rms(y[:, c:c + HEAD_DIM], cg_ref[:, cols])


def _norm_matmul(x, gain, w, *, mode, out_dtype, tiles, layer=0, col_gain=None, emit_w=False):
    m, k = x.shape
    n = w.shape[-1]
    tm, tn = tiles.tm, tiles.tn_full_k
    fused_norm = gain is not None
    assert not emit_w or m == tm
    assert fused_norm or x.dtype == BF16
    if col_gain is None:
        col_gain = jnp.ones((1, n), F32)
    body = functools.partial(_norm_matmul_body, mode=mode, tm=tm, tn=tn, emit_w=emit_w, fused_norm=fused_norm)
    out_specs = pl.BlockSpec((tm, tn), lambda i, j: (i, j))
    out_shape = jax.ShapeDtypeStruct((m, n), out_dtype)
    if emit_w:
        out_specs = [out_specs, pl.BlockSpec((k, tn), lambda i, j: (0, j))]
        out_shape = [out_shape, jax.ShapeDtypeStruct((k, n), BF16)]
    if fused_norm:
        x_specs = [pl.BlockSpec((tm, k), lambda i, j: (i, 0), pipeline_mode=pl.Buffered(1)),
                   pl.BlockSpec((1, k), lambda i, j: (0, 0))]
        x_args = (x, gain.reshape(1, k))
        scratch = [pltpu.VMEM((tm, k), BF16)]
    else:
        x_specs = [pl.BlockSpec((tm, k), lambda i, j: (i, 0))]
        x_args = (x,)
        scratch = []
    return pl.pallas_call(
        body,
        grid=(m // tm, n // tn),
        in_specs=x_specs + [
            _weight_spec(w, layer, (k, tn), lambda i, j: (0, j)),
            pl.BlockSpec((1, tn), lambda i, j: (0, j)),
        ],
        out_specs=out_specs,
        out_shape=out_shape,
        scratch_shapes=scratch,
        compiler_params=_params("arbitrary", "arbitrary"),
        name=f"norm_matmul_{mode}_{tm}",
    )(*x_args, w, col_gain)


def _matmul_res_body(a_ref, w_ref, r_ref, o_ref, *rest, emit_w):
    kk = pl.program_id(2)
    if emit_w:
        (wb_ref,) = rest
        wb_ref[...] = w_ref[...].astype(BF16)
        w_src = wb_ref
    else:
        w_src = w_ref
    d = jnp.dot(a_ref[...], w_src[...], preferred_element_type=F32)

    @pl.when(kk == 0)
    def _():
        o_ref[...] = r_ref[...] + d

    @pl.when(kk != 0)
    def _():
        o_ref[...] += d


def _matmul_res(a, w, res, *, tiles, layer=0, emit_w=False):
    m, k = a.shape
    n = w.shape[-1]
    tm, tn, tk = tiles.tm, tiles.tn_res, min(tiles.tk_res, k)
    assert not emit_w or m == tm
    out_specs = pl.BlockSpec((tm, tn), lambda i, j, kk: (i, j))
    out_shape = jax.ShapeDtypeStruct((m, n), F32)
    if emit_w:
        out_specs = [out_specs, pl.BlockSpec((tk, tn), lambda i, j, kk: (kk, j))]
        out_shape = [out_shape, jax.ShapeDtypeStruct((k, n), BF16)]
    return pl.pallas_call(
        functools.partial(_matmul_res_body, emit_w=emit_w),
        grid=(m // tm, n // tn, k // tk),
        in_specs=[
            pl.BlockSpec((tm, tk), lambda i, j, kk: (i, kk)),
            _weight_spec(w, layer, (tk, tn), lambda i, j, kk: (kk, j)),
            pl.BlockSpec((tm, tn), lambda i, j, kk: (i, j)),
        ],
        out_specs=out_specs,
        out_shape=out_shape,
        compiler_params=_params("arbitrary", "arbitrary", "arbitrary"),
        name=f"matmul_res_{tm}_{k}",
    )(a, w, res)


def _rel_bucket(dist):
    max_exact = N_BUCKETS // 2
    d32 = jnp.maximum(dist, 1).astype(F32)
    large = max_exact + (jnp.log(d32 / max_exact) / math.log(MAX_DISTANCE / max_exact)
                         * (N_BUCKETS - max_exact)).astype(jnp.int32)
    large = jnp.minimum(large, N_BUCKETS - 1)
    return jnp.where(dist < max_exact, dist, large)


def _bias_by_stride(bias_table):
    m = jnp.arange(BAND + 1)
    return jnp.stack([bias_table[:, g][_rel_bucket(m * d)] for g, (_, d) in enumerate(DILATED_PAIRS)])


def _prompt_bias(bias_m):
    period = 2 * BAND + 1
    u = jnp.concatenate([bias_m[:, ::-1, :], jnp.full((N_GROUPS, period - BAND - 1, N_A_HEADS), MASKED, F32)],
                        axis=1).transpose(0, 2, 1)
    flat = jnp.tile(u, (1, 1, BAND))[:, :, :BAND * 2 * BAND]
    return flat.reshape(N_GROUPS, N_A_HEADS, BAND, 2 * BAND)


def _attn_a_prompt_body(q0, k0, v0, q1, k1, v1, q2, k2, v2, bias_ref, o_ref, og_ref, lse_ref, *, t):
    qkv = ((q0, k0, v0), (q1, k1, v1), (q2, k2, v2))
    ones = jnp.ones((BAND, HEAD_DIM), BF16)
    dn = (((1,), (1,)), ((), ()))

    def rows_of(start, d):
        if d == 1:
            return pl.ds(start, BAND)
        return pl.ds(start, BAND, stride=d)

    def blocks(g, specs):
        q_ref, k_ref, v_ref = qkv[g]
        with_prev = specs[0][1] is not None

        def logits(q, rows, bias):
            return lax.dot_general(q, k_ref[rows, :].astype(BF16), dn, preferred_element_type=F32) * ATT_SCALE + bias

        def weighted(p, rows):
            vx = jnp.concatenate([v_ref[rows, :].astype(BF16), ones], axis=1)
            return jnp.dot(p.astype(BF16), vx, preferred_element_type=F32)

        qs = [q_ref[cur, :].astype(BF16) for cur, _ in specs]
        s_cs = [logits(q, cur, bias_ref[g, :, BAND:]) for q, (cur, _) in zip(qs, specs)]
        mxs = [jnp.max(s, axis=-1, keepdims=True) for s in s_cs]
        if with_prev:
            s_ps = [logits(q, prev, bias_ref[g, :, :BAND]) for q, (_, prev) in zip(qs, specs)]
            mxs = [jnp.maximum(mx, jnp.max(s, axis=-1, keepdims=True)) for mx, s in zip(mxs, s_ps)]
        accs = [weighted(jnp.exp(s - mx), cur) for s, mx, (cur, _) in zip(s_cs, mxs, specs)]
        if with_prev:
            accs = [acc + weighted(jnp.exp(s - mx), prev) for acc, s, mx, (_, prev) in zip(accs, s_ps, mxs, specs)]
        for acc, mx, (cur, _) in zip(accs, mxs, specs):
            den = acc[:, HEAD_DIM:]
            og_ref[g, cur, :] = acc[:, :HEAD_DIM] / den
            lse_ref[g, cur, :] = mx + jnp.log(den)

    for g, (_, d) in enumerate(DILATED_PAIRS):
        span = BAND * d
        n_blk = t // span
        if n_blk == 1:
            per = 8

            def residue_step(it, carry, g=g, d=d, per=per):
                blocks(g, [(rows_of(it * per + u, d), None) for u in range(per)])
                return carry
            lax.fori_loop(0, d // per, residue_step, 0)
        elif d == 1:
            per = 5
            assert (n_blk - 1) % per == 0
            blocks(g, [(rows_of(0, d), None)])

            def blk_step(it, carry, g=g, d=d, per=per, span=span):
                starts = [pl.multiple_of((1 + it * per + u) * span, BAND) for u in range(per)]
                blocks(g, [(rows_of(st, d), rows_of(st - span, d)) for st in starts])
                return carry
            lax.fori_loop(0, (n_blk - 1) // per, blk_step, 0)
        else:
            blocks(g, [(rows_of(r, d), None) for r in range(d)])

            def blk_step(n, carry, g=g, d=d, span=span):
                blocks(g, [(rows_of(n * span + r, d), rows_of(n * span + r - span, d)) for r in range(d)])
                return carry
            lax.fori_loop(1, n_blk, blk_step, 0)

    def merge_step(c, carry):
        rows = pl.ds(pl.multiple_of(c * BAND, BAND), BAND)
        l0, l1, l2 = lse_ref[0, rows, :], lse_ref[1, rows, :], lse_ref[2, rows, :]
        mx = jnp.maximum(jnp.maximum(l0, l1), l2)
        e0, e1, e2 = jnp.exp(l0 - mx), jnp.exp(l1 - mx), jnp.exp(l2 - mx)
        tot = e0 + e1 + e2
        o = (e0 / tot) * og_ref[0, rows, :] + (e1 / tot) * og_ref[1, rows, :] + (e2 / tot) * og_ref[2, rows, :]
        o_ref[rows, :] = o.astype(o_ref.dtype)
        return carry

    lax.fori_loop(0, t // BAND, merge_step, 0)


def _attn_a_prompt(qkv, bias, *, b, t):
    heads = N_A_HEADS

    def col_spec(g, c):
        base = (g * 3 + c) * heads
        return pl.BlockSpec((t, HEAD_DIM), lambda bi, h: (bi, base + h))

    in_specs = [col_spec(g, c) for g in range(N_GROUPS) for c in range(3)]
    in_specs.append(pl.BlockSpec((N_GROUPS, None, BAND, 2 * BAND), lambda bi, h: (0, h, 0, 0)))
    return pl.pallas_call(
        functools.partial(_attn_a_prompt_body, t=t),
        grid=(b, heads),
        in_specs=in_specs,
        out_specs=pl.BlockSpec((t, HEAD_DIM), lambda bi, h: (bi, h)),
        out_shape=jax.ShapeDtypeStruct((b * t, A_WIDTH), BF16),
        scratch_shapes=[pltpu.VMEM((N_GROUPS, t, HEAD_DIM), F32),
                        pltpu.VMEM((N_GROUPS, t, HEAD_DIM), F32)],
        compiler_params=_params("arbitrary", "arbitrary"),
        name="attn_a_prompt",
    )(*([qkv] * 9), bias)


def _attn_a_sample_body(qkv_ref, c0_ref, c1_ref, c2_ref, bb0_ref, bb12_ref, bn0_ref, bn12_ref, o_ref):
    s = pl.program_id(1)
    outs, lses = [], []
    for g in range(N_GROUPS):
        q = qkv_ref[s, g, 0]
        if g == 0:
            kb, vb, bias_b = c0_ref[:, 0], c0_ref[:, 1], bb0_ref[...]
            kn, vn, bias_n = qkv_ref[:, 0, 1], qkv_ref[:, 0, 2], bn0_ref[...]
        else:
            c_ref = c1_ref if g == 1 else c2_ref
            kb, vb, bias_b = c_ref[:, 0], c_ref[:, 1], bb12_ref[g - 1]
            kn, vn, bias_n = qkv_ref[pl.ds(s, 1), g, 1], qkv_ref[pl.ds(s, 1), g, 2], bn12_ref[g - 1][None]
        lb = jnp.sum(q[None] * kb, axis=-1, keepdims=True) * ATT_SCALE + bias_b
        ln = jnp.sum(q[None] * kn, axis=-1, keepdims=True) * ATT_SCALE + bias_n
        mx = jnp.maximum(jnp.max(lb, axis=0), jnp.max(ln, axis=0))
        pb = jnp.exp(lb - mx[None])
        pn = jnp.exp(ln - mx[None])
        den = jnp.sum(pb, axis=0) + jnp.sum(pn, axis=0)
        acc = jnp.sum(pb * vb, axis=0) + jnp.sum(pn * vn, axis=0)
        outs.append(acc / den)
        lses.append(mx + jnp.log(den))
    mx = jnp.maximum(jnp.maximum(lses[0], lses[1]), lses[2])
    es = [jnp.exp(l - mx) for l in lses]
    tot = es[0] + es[1] + es[2]
    o_ref[...] = (es[0] / tot) * outs[0] + (es[1] / tot) * outs[1] + (es[2] / tot) * outs[2]


def _attn_a_sample(qkv_s, caches, bias_m, *, bs, s_new):
    h = N_A_HEADS
    lane = (h, HEAD_DIM)

    def dense(x):
        return jnp.broadcast_to(x[..., None], x.shape + (HEAD_DIM,)).astype(F32)

    sq = jnp.arange(s_new)
    r = jnp.arange(BAND)
    m_b = BAND + sq[:, None] - r[None, :]
    bb0 = dense(jnp.where((m_b <= BAND)[..., None], bias_m[0][jnp.clip(m_b, 0, BAND)], MASKED))
    m_n = sq[:, None] - sq[None, :]
    bn0 = dense(jnp.where((m_n >= 0)[..., None], bias_m[0][jnp.clip(m_n, 0, BAND)], MASKED))
    bb12 = dense(jnp.stack([bias_m[g][BAND - r] for g in (1, 2)]))
    bn12 = dense(jnp.stack([bias_m[g][0] for g in (1, 2)]))

    c0 = caches[0]
    c1 = caches[1].reshape(bs, BAND, DILATED_PAIRS[1][1], 2, h, HEAD_DIM)
    c2 = caches[2].reshape(bs, BAND, DILATED_PAIRS[2][1], 2, h, HEAD_DIM)
    return pl.pallas_call(
        _attn_a_sample_body,
        grid=(bs, s_new),
        in_specs=[
            pl.BlockSpec((None, s_new, N_GROUPS, 3) + lane, lambda b, s: (b, 0, 0, 0, 0, 0)),
            pl.BlockSpec((None, BAND, 2) + lane, lambda b, s: (b, 0, 0, 0, 0)),
            pl.BlockSpec((None, BAND, None, 2) + lane, lambda b, s: (b, 0, s, 0, 0, 0)),
            pl.BlockSpec((None, BAND, None, 2) + lane, lambda b, s: (b, 0, s, 0, 0, 0)),
            pl.BlockSpec((None, BAND) + lane, lambda b, s: (s, 0, 0, 0)),
            pl.BlockSpec((2, BAND) + lane, lambda b, s: (0, 0, 0, 0)),
            pl.BlockSpec((None, s_new) + lane, lambda b, s: (s, 0, 0, 0)),
            pl.BlockSpec((2,) + lane, lambda b, s: (0, 0, 0)),
        ],
        out_specs=pl.BlockSpec((None, None) + lane, lambda b, s: (b, s, 0, 0)),
        out_shape=jax.ShapeDtypeStruct((bs, s_new) + lane, F32),
        compiler_params=_params("arbitrary", "arbitrary"),
        name="attn_a_sample",
    )(qkv_s, c0, c1, c2, bb0, bb12, bn0, bn12)


LOG2E = 1.0 / math.log(2.0)
SB_LOGIT_SCALE = ATT_SCALE * LOG2E


def _softplus2(z2):
    return jnp.maximum(z2, 0.0) + jnp.log2(1.0 + jnp.exp2(-jnp.abs(z2)))


def _suffix_matrix():
    n = 2 * PAGE_SIZE
    j = lax.broadcasted_iota(jnp.int32, (n, n), 0) % PAGE_SIZE
    c = lax.broadcasted_iota(jnp.int32, (n, n), 1)
    return jnp.where((c >= PAGE_SIZE) | (j > c), -1.0, 0.0).astype(BF16)


def _hi_lo(l):
    hi = l.astype(BF16)
    lo = (l - hi.astype(F32)).astype(BF16)
    return jnp.concatenate([hi, lo], axis=1)


def _attn_b_prompt_body(q_ref, k_ref, v_ref, bias_ref, o_ref, kb_ref, vb_ref, *, t, tq):
    i = pl.program_id(2)

    @pl.when(i == 0)
    def _():
        def cast_step(c, carry):
            rows = pl.ds(pl.multiple_of(c * 256, 256), 256)
            kb_ref[rows, :] = (k_ref[rows, :] * SB_LOGIT_SCALE).astype(BF16)
            vb_ref[rows, :] = v_ref[rows, :].astype(BF16)
            return carry
        lax.fori_loop(0, t // 256, cast_step, 0)

    suffix = _suffix_matrix()
    dn = (((1,), (1,)), ((), ()))
    heads = range(GQ)

    def block_rows(jlast, nblk):
        return pl.ds(pl.multiple_of((jlast - (nblk - 1)) * PAGE_SIZE, PAGE_SIZE), nblk * PAGE_SIZE)

    def logits(rows, nblk):
        k = kb_ref[rows, :]
        return [lax.dot_general(q_ref[:, g * HEAD_DIM:(g + 1) * HEAD_DIM], k, dn, preferred_element_type=F32)
                + bias_ref[g, :, :nblk * PAGE_SIZE] for g in heads]

    def suffix_sums(zs, nblk, visible):
        sps = [_softplus2(z) for z in zs]
        ls = [sp if visible is None else jnp.where(visible, sp, 0.0) for sp in sps]
        ss = [jnp.dot(jnp.concatenate([_hi_lo(l[:, u * PAGE_SIZE:(u + 1) * PAGE_SIZE])
                                       for u in reversed(range(nblk))], axis=0),
                      suffix, preferred_element_type=F32) for l in ls]
        return sps, ss

    def weights(zs, sps, ss, runs, nblk, visible):
        avals, new_runs = [], []
        for g, (z, sp, s) in enumerate(zip(zs, sps, ss)):
            run = None if runs is None else runs[g]
            later = [None] * nblk
            for idx in range(nblk):
                su = s[idx * tq:(idx + 1) * tq]
                later[nblk - 1 - idx] = su[:, :PAGE_SIZE] if run is None else su[:, :PAGE_SIZE] + run
                run = su[:, PAGE_SIZE:] if run is None else run + su[:, PAGE_SIZE:]
            a = jnp.exp2(z - sp + (later[0] if nblk == 1 else jnp.concatenate(later, axis=1)))
            if visible is not None:
                a = jnp.where(visible, a, 0.0)
            avals.append(a.astype(BF16))
            new_runs.append(run)
        return avals, tuple(new_runs)

    def weighted_values(avals, rows):
        v = vb_ref[rows, :]
        return [jnp.dot(a, v, preferred_element_type=F32) for a in avals]

    def sweep(jlast, nblk, state):
        rows = block_rows(jlast, nblk)
        visible = None
        if state is None:
            qpos = lax.broadcasted_iota(jnp.int32, (tq, nblk * PAGE_SIZE), 0)
            kpos = lax.broadcasted_iota(jnp.int32, (tq, nblk * PAGE_SIZE), 1) - (nblk - 1) * PAGE_SIZE
            visible = kpos < qpos
        zs = logits(rows, nblk)
        sps, ss = suffix_sums(zs, nblk, visible)
        avals, runs = weights(zs, sps, ss, None if state is None else state[0], nblk, visible)
        pvs = weighted_values(avals, rows)
        accs = pvs if state is None else [acc + pv for acc, pv in zip(state[1], pvs)]
        return runs, tuple(accs)

    total = i + 1
    n_groups = lax.shift_right_logical(total * 11, 5)
    rem = total - n_groups * KEY_BLOCKS_PER_SWEEP
    state = lax.cond(rem == 1, lambda: sweep(i, 1, None),
                     lambda: lax.cond(rem == 2, lambda: sweep(i, 2, None),
                                      lambda: sweep(i, KEY_BLOCKS_PER_SWEEP, None)))
    first_size = jnp.where(rem == 0, KEY_BLOCKS_PER_SWEEP, rem)
    n_rest = n_groups - jnp.where(rem == 0, 1, 0)
    first = i - first_size
    _, accs = lax.fori_loop(0, n_rest,
                            lambda p, st: sweep(first - KEY_BLOCKS_PER_SWEEP * p, KEY_BLOCKS_PER_SWEEP, st), state)
    for g in heads:
        o_ref[:, g * HEAD_DIM:(g + 1) * HEAD_DIM] = accs[g].astype(o_ref.dtype)


def _attn_b_prompt(q, kv, sb, *, b, t):
    tq = PAGE_SIZE
    nq = t // tq
    assert KEY_BLOCKS_PER_SWEEP == 3 and nq <= 16
    bias = jnp.broadcast_to(sb[:, :, None, None], (N_KV_HEADS, GQ, 1, KEY_BLOCKS_PER_SWEEP * PAGE_SIZE))
    return pl.pallas_call(
        functools.partial(_attn_b_prompt_body, t=t, tq=tq),
        grid=(b, N_KV_HEADS, nq),
        in_specs=[
            pl.BlockSpec((tq, GQ * HEAD_DIM), lambda bi, kh, i: (bi * nq + i, kh)),
            pl.BlockSpec((t, HEAD_DIM), lambda bi, kh, i: (bi, kh)),
            pl.BlockSpec((t, HEAD_DIM), lambda bi, kh, i: (bi, N_KV_HEADS + kh)),
            pl.BlockSpec((None, GQ, 1, KEY_BLOCKS_PER_SWEEP * PAGE_SIZE), lambda bi, kh, i: (kh, 0, 0, 0)),
        ],
        out_specs=pl.BlockSpec((tq, GQ * HEAD_DIM), lambda bi, kh, i: (bi * nq + i, kh)),
        out_shape=jax.ShapeDtypeStruct((b * t, N_B_HEADS * HEAD_DIM), BF16),
        scratch_shapes=[pltpu.VMEM((t, HEAD_DIM), BF16), pltpu.VMEM((t, HEAD_DIM), BF16)],
        compiler_params=_params("arbitrary", "arbitrary", "arbitrary"),
        name="attn_b_prompt",
    )(q, kv, kv, bias)


def _attn_b_sample_body(pt_ref, q_ref, kvn_ref, bias_ref, *rest, s_new, n_steps):
    page_refs = rest[:PAGES_PER_STEP]
    o_ref, run_ref, acc_ref = rest[PAGES_PER_STEP:]
    p = pl.program_id(1)
    rows = GQ * s_new
    row_tok = lax.broadcasted_iota(jnp.int32, (rows, HEAD_DIM), 0) % s_new

    @pl.when(p == 0)
    def _():
        for kh in range(N_KV_HEADS):
            q = q_ref[kh]
            run = jnp.zeros((rows, HEAD_DIM), F32)
            acc = jnp.zeros((rows, HEAD_DIM), F32)
            for j in reversed(range(s_new)):
                kj = kvn_ref[0, kh, pl.ds(j, 1), :]
                vj = kvn_ref[1, kh, pl.ds(j, 1), :]
                z = jnp.sum(q * kj, axis=-1, keepdims=True) * SB_LOGIT_SCALE + bias_ref[kh]
                valid = row_tok > j
                sp = _softplus2(z)
                a = jnp.where(valid, jnp.exp2(z - sp + run), 0.0)
                acc = acc + a * vj
                run = run + jnp.where(valid, -sp, 0.0)
            run_ref[kh] = run
            acc_ref[kh] = acc

    suffix = _suffix_matrix()
    stride = 2 * N_KV_HEADS
    dn = (((1,), (1,)), ((), ()))
    def head_rows(first):
        return jnp.concatenate([r[pl.ds(first, PAGE_SIZE, stride=stride), :] for r in page_refs], axis=0).astype(BF16)

    def later_and_run(s, run):
        later = []
        for u in range(PAGES_PER_STEP):
            su = s[u * rows:(u + 1) * rows]
            later.append(su[:, :PAGE_SIZE] + run)
            run = run + su[:, PAGE_SIZE:]
        return jnp.concatenate(later, axis=1), run

    heads = range(N_KV_HEADS)
    zs = [lax.dot_general(q_ref[kh].astype(BF16), head_rows(kh), dn, preferred_element_type=F32) * SB_LOGIT_SCALE
          + jnp.tile(bias_ref[kh], (1, PAGES_PER_STEP)) for kh in heads]
    sps = [_softplus2(z) for z in zs]
    ss = [jnp.dot(jnp.concatenate([_hi_lo(sp[:, u * PAGE_SIZE:(u + 1) * PAGE_SIZE])
                                   for u in range(PAGES_PER_STEP)], axis=0),
                  suffix, preferred_element_type=F32) for sp in sps]
    lrs = [later_and_run(s, run_ref[kh]) for kh, s in zip(heads, ss)]
    avals = [jnp.exp2(z - sp + later) for z, sp, (later, _) in zip(zs, sps, lrs)]
    pvs = [jnp.dot(a.astype(BF16), head_rows(N_KV_HEADS + kh), preferred_element_type=F32)
           for kh, a in zip(heads, avals)]
    for kh in heads:
        acc_ref[kh] += pvs[kh]
        run_ref[kh] = lrs[kh][1]

    @pl.when(p == n_steps - 1)
    def _():
        o_ref[...] = acc_ref[...]


def _attn_b_sample(q_rows, kv_new, sb, cache_kv, page_table, *, s_new):
    bs, n_pages = page_table.shape
    assert n_pages % PAGES_PER_STEP == 0
    n_steps = n_pages // PAGES_PER_STEP
    rows = GQ * s_new
    bias = jnp.broadcast_to(sb[:, :, None, None], (N_KV_HEADS, GQ, s_new, HEAD_DIM)).reshape(
        N_KV_HEADS, rows, HEAD_DIM)
    pool = cache_kv.reshape(cache_kv.shape[0], PAGE_SIZE * 2 * N_KV_HEADS, HEAD_DIM)

    def page_spec(u):
        return pl.BlockSpec((None, PAGE_SIZE * 2 * N_KV_HEADS, HEAD_DIM),
                            lambda b, p, pt: (pt[b, n_pages - 1 - (p * PAGES_PER_STEP + u)], 0, 0))

    grid_spec = pltpu.PrefetchScalarGridSpec(
        num_scalar_prefetch=1,
        grid=(bs, n_steps),
        in_specs=[
            pl.BlockSpec((None, N_KV_HEADS, rows, HEAD_DIM), lambda b, p, pt: (b, 0, 0, 0)),
            pl.BlockSpec((None, 2, N_KV_HEADS, s_new, HEAD_DIM), lambda b, p, pt: (b, 0, 0, 0, 0)),
            pl.BlockSpec((N_KV_HEADS, rows, HEAD_DIM), lambda b, p, pt: (0, 0, 0)),
        ] + [page_spec(u) for u in range(PAGES_PER_STEP)],
        out_specs=pl.BlockSpec((None, N_KV_HEADS, rows, HEAD_DIM), lambda b, p, pt: (b, 0, 0, 0)),
        scratch_shapes=[pltpu.VMEM((N_KV_HEADS, rows, HEAD_DIM), F32),
                        pltpu.VMEM((N_KV_HEADS, rows, HEAD_DIM), F32)],
    )
    return pl.pallas_call(
        functools.partial(_attn_b_sample_body, s_new=s_new, n_steps=n_steps),
        grid_spec=grid_spec,
        out_shape=jax.ShapeDtypeStruct((bs, N_KV_HEADS, rows, HEAD_DIM), F32),
        compiler_params=_params("arbitrary", "arbitrary"),
        name="attn_b_sample",
    )(page_table, q_rows, kv_new, bias, *([pool] * PAGES_PER_STEP))


def _roll_rows(buf, new):
    wb, s = buf.shape[1], new.shape[1]
    zero = jnp.zeros((), buf.dtype)
    shift_cfg = [(0, 0, 0)] * buf.ndim
    shift_cfg[1] = (-s, s, 0)
    new_cfg = [(0, 0, 0)] * buf.ndim
    new_cfg[1] = (wb - s, 0, 0)
    row = lax.broadcasted_iota(jnp.int32, buf.shape, 1)
    return jnp.where(row >= wb - s, lax.pad(new, zero, new_cfg), lax.pad(buf, zero, shift_cfg))


def _mlp(x, gain, w_up, w_down, *, tiles, layer=0, emit_w=False):
    if emit_w:
        h, w_up_b = _norm_matmul(x, gain, w_up, mode="relu2", out_dtype=BF16, tiles=tiles, layer=layer, emit_w=True)
        y, w_down_b = _matmul_res(h, w_down, x, tiles=tiles, layer=layer, emit_w=True)
        return y, w_up_b, w_down_b
    (xn,) = _rms_norm(x, [gain], tr=tiles.tr_norm)
    h = _norm_matmul(xn, None, w_up, mode="relu2", out_dtype=BF16, tiles=tiles)
    return _matmul_res(h, w_down, x, tiles=tiles)


@jax.jit
def _step(x_prompt, x_sample, cache_win_g0, cache_win_g1, cache_win_g2, cache_kv, page_table, bias_table,
          norm_mix, norm_mlp, w_a_in, g_a_q, g_a_k, w_a_out, norm_kv, w_kv, w_b_q, b_sb, w_b_out, w_up, w_down):
    bp, tp, _ = x_prompt.shape
    bs, ts, _ = x_sample.shape
    pt, st = PROMPT_TILES, _sample_tiles(bs * ts)
    xp = x_prompt.reshape(bp * tp, D_MODEL)
    xs = x_sample.reshape(bs * ts, D_MODEL)
    caches = (cache_win_g0[0], cache_win_g1[0], cache_win_g2[0])

    head_gain = jnp.concatenate([jnp.tile(g_a_q[0], N_A_HEADS), jnp.tile(g_a_k[0], N_A_HEADS),
                                 jnp.ones((A_WIDTH,), F32)])
    col_gain = jnp.tile(head_gain, N_GROUPS).reshape(1, -1).astype(F32)
    qkv_s, w_in = _norm_matmul(xs, norm_mix[0], w_a_in, mode="qkv", out_dtype=F32, tiles=st,
                               col_gain=col_gain, emit_w=True)
    (xn_p,) = _rms_norm(xp, [norm_mix[0]], tr=pt.tr_norm)
    qkv_p = _norm_matmul(xn_p, None, w_in, mode="qkv", out_dtype=F32, tiles=pt, col_gain=col_gain)

    bias_m = _bias_by_stride(bias_table)
    qkv_s6 = qkv_s.reshape(bs, ts, N_GROUPS, 3, N_A_HEADS, HEAD_DIM)
    o_s = _attn_a_sample(qkv_s6, caches, bias_m, bs=bs, s_new=ts)
    o_s = o_s.reshape(bs * ts, A_WIDTH).astype(BF16)
    o_p = _attn_a_prompt(qkv_p, _prompt_bias(bias_m), b=bp, t=tp)

    xs, w_ao = _matmul_res(o_s, w_a_out, xs, tiles=st, emit_w=True)
    xp = _matmul_res(o_p, w_ao, xp, tiles=pt)
    xs, w_up0, w_dn0 = _mlp(xs, norm_mlp[0], w_up, w_down, tiles=st, layer=0, emit_w=True)
    qkv_p_windows, xs = lax.optimization_barrier((qkv_p, xs))
    xp = _mlp(xp, norm_mlp[0], w_up0, w_dn0, tiles=pt)

    kv_s, w_kvb = _norm_matmul(xs, norm_kv, w_kv, mode="plain", out_dtype=F32, tiles=st, emit_w=True)
    xn_kv, xn_bq = _rms_norm(xp, [norm_kv, norm_mix[1]], tr=pt.tr_norm)
    kv_p = _norm_matmul(xn_kv, None, w_kvb, mode="plain", out_dtype=F32, tiles=pt)

    qb_s, w_bq = _norm_matmul(xs, norm_mix[1], w_b_q, mode="plain", out_dtype=F32, tiles=st, emit_w=True)
    qb_p = _norm_matmul(xn_bq, None, w_bq, mode="plain", out_dtype=BF16, tiles=pt)

    sb = b_sb[0].astype(F32).reshape(N_KV_HEADS, GQ) * LOG2E
    q_rows = qb_s.reshape(bs, ts, N_KV_HEADS, GQ, HEAD_DIM).transpose(0, 2, 3, 1, 4).reshape(
        bs, N_KV_HEADS, GQ * ts, HEAD_DIM)
    kv_s5 = kv_s.reshape(bs, ts, 2, N_KV_HEADS, HEAD_DIM)
    ob_s = _attn_b_sample(q_rows, kv_s5.transpose(0, 2, 3, 1, 4), sb, cache_kv, page_table, s_new=ts)
    ob_s = ob_s.reshape(bs, N_KV_HEADS, GQ, ts, HEAD_DIM).transpose(0, 3, 1, 2, 4).reshape(
        bs * ts, N_B_HEADS * HEAD_DIM).astype(BF16)
    ob_p = _attn_b_prompt(qb_p, kv_p, sb, b=bp, t=tp)

    xs, w_bo = _matmul_res(ob_s, w_b_out, xs, tiles=st, emit_w=True)
    xp = _matmul_res(ob_p, w_bo, xp, tiles=pt)
    xs, w_up1, w_dn1 = _mlp(xs, norm_mlp[1], w_up, w_down, tiles=st, layer=1, emit_w=True)
    xp = _mlp(xp, norm_mlp[1], w_up1, w_dn1, tiles=pt)

    qkv_p6 = qkv_p_windows.reshape(bp, tp, N_GROUPS, 3, N_A_HEADS, HEAD_DIM)
    win_p = [qkv_p6[:, tp - min(window, tp):, g, 1:3][None] for g, (window, _) in enumerate(DILATED_PAIRS)]
    win_s = [_roll_rows(caches[g], qkv_s6[:, :, g, 1:3])[None] for g in range(N_GROUPS)]
    return (xp.reshape(bp, tp, D_MODEL), xs.reshape(bs, ts, D_MODEL), *win_p, *win_s,
            kv_p.reshape(bp, tp, 2, N_KV_HEADS, HEAD_DIM), kv_s5)


def kernel(x_prompt, x_sample, cache_win_g0, cache_win_g1, cache_win_g2, cache_kv, page_table, bias_table,
           norm_mix, norm_mlp, w_a_in, g_a_q, g_a_k, w_a_out, norm_kv, w_kv, w_b_q, b_sb, w_b_out, w_up, w_down):
    return _step(x_prompt, x_sample, cache_win_g0, cache_win_g1, cache_win_g2, cache_kv, page_table, bias_table,
                 norm_mix, norm_mlp, w_a_in, g_a_q, g_a_k, w_a_out, norm_kv, w_kv, w_b_q, b_sb, w_b_out,
                 w_up, w_down)
```
